```python
import jax, jax.numpy as jnp
from jax import lax
import numpy as np

D_MODEL = 1024
BATCH = 32
SEQ = 256
DEPTH = 4
DEC_BATCH = 4
DEC_SEQ = 1024
PAST_LEN = 512

GRID_W = 64
N_HEADS = 8
KV_HEADS = 2
HEAD_DIM = 64
Q_GROUP = N_HEADS // KV_HEADS
WINDOW = 128
BLOCK = 128
ROPE_THETA = 10000.0
R_HEADS = 8
R_HEAD = 64
RW = R_HEADS * R_HEAD
LORA_W = 64
LORA_A = 64
LORA_G = 128
ATT_Q = N_HEADS * HEAD_DIM
ATT_KV = KV_HEADS * HEAD_DIM
D_IN = ATT_Q + 2 * ATT_KV + 3 * RW
MIX_W = ATT_Q + RW
D_FF = 2816
N_EXPERTS = 8
TOP_K = 2
D_FF_EXPERT = 1408
N_DENSE = (DEPTH + 1) // 2
N_MOE = DEPTH // 2
NORM_EPS = 1e-6
GN_EPS = 64e-5
NEG_INF = -1e30

kernel_name = 'hybrid_dit_rwkv7_swa_prefix_step'


def rmsnorm(x, g):
    xf = x.astype(jnp.float32)
    y = xf * lax.rsqrt(jnp.mean(xf * xf, -1, keepdims=True) + NORM_EPS)
    return (y * g.astype(jnp.float32)).astype(x.dtype)


def modulation(cond, w_mod, b_mod):
    m = jax.nn.silu(cond) @ w_mod + b_mod
    return jnp.split(m, 6, axis=-1)


def centred_shift(u, mu):
    prev = jnp.pad(u[:, :-1], ((0, 0), (1, 0), (0, 0)))
    nxt = jnp.pad(u[:, 1:], ((0, 0), (0, 1), (0, 0)))
    return u + mu * (0.5 * (prev + nxt) - u)


def axial_rope(x, row, col):
    half = HEAD_DIM // 2
    nf = half // 2
    inv = ROPE_THETA ** (-jnp.arange(nf, dtype=jnp.float32) / nf)

    def rot(xa, pos):
        ang = pos.astype(jnp.float32)[:, None] * inv[None, :]
        cos = jnp.cos(ang)[:, None, :]
        sin = jnp.sin(ang)[:, None, :]
        x1 = xa[..., :nf].astype(jnp.float32)
        x2 = xa[..., nf:].astype(jnp.float32)
        return jnp.concatenate([x1 * cos - x2 * sin, x1 * sin + x2 * cos], -1)

    out = jnp.concatenate([rot(x[..., :half], row), rot(x[..., half:], col)], -1)
    return out.astype(x.dtype)


def sink_softmax(logits, sink):
    s = jnp.broadcast_to(sink[None, :, :, None, None], logits.shape[:-1] + (1,))
    p = jax.nn.softmax(jnp.concatenate([s, logits], -1), axis=-1)
    return p[..., 1:]


def attn_context(q, k, v, sink):
    B, S = q.shape[:2]
    nb = S // BLOCK
    qb = jnp.moveaxis(q.reshape(B, nb, BLOCK, KV_HEADS, Q_GROUP, HEAD_DIM), 1, 0)
    scale = HEAD_DIM ** -0.5

    def blk(qblk):
        s = jnp.einsum('bqkgd,bskd->bkgqs', qblk, k, preferred_element_type=jnp.float32) * scale
        p = sink_softmax(s, sink).astype(v.dtype)
        return jnp.einsum('bkgqs,bskd->bqkgd', p, v)

    o = lax.map(blk, qb)
    return jnp.moveaxis(o, 0, 1).reshape(B, S, ATT_Q)


def attn_latent(q, k, v, ck, cv, sink):
    B, T = q.shape[:2]
    nb = T // BLOCK
    span = BLOCK + 2 * WINDOW
    P = ck.shape[1]
    kp = jnp.pad(k, ((0, 0), (WINDOW, WINDOW), (0, 0), (0, 0)))
    vp = jnp.pad(v, ((0, 0), (WINDOW, WINDOW), (0, 0), (0, 0)))
    scale = HEAD_DIM ** -0.5

    def blk(i):
        start = i * BLOCK
        qblk = lax.dynamic_slice_in_dim(q, start, BLOCK, 1).reshape(B, BLOCK, KV_HEADS, Q_GROUP, HEAD_DIM)
        kw = lax.dynamic_slice_in_dim(kp, start, span, 1)
        vw = lax.dynamic_slice_in_dim(vp, start, span, 1)
        qpos = start + jnp.arange(BLOCK)
        kpos = start - WINDOW + jnp.arange(span)
        ok = (kpos >= 0)[None, :] & (kpos < T)[None, :] & (jnp.abs(qpos[:, None] - kpos[None, :]) <= WINDOW)
        s_c = jnp.einsum('bqkgd,bskd->bkgqs', qblk, ck, preferred_element_type=jnp.float32) * scale
        s_w = jnp.einsum('bqkgd,bskd->bkgqs', qblk, kw, preferred_element_type=jnp.float32) * scale
        s_w = jnp.where(ok, s_w, NEG_INF)
        p = sink_softmax(jnp.concatenate([s_c, s_w], -1), sink).astype(v.dtype)
        return (jnp.einsum('bkgqs,bskd->bqkgd', p[..., :P], cv)
                + jnp.einsum('bkgqs,bskd->bqkgd', p[..., P:], vw))

    o = lax.map(blk, jnp.arange(nb))
    return jnp.moveaxis(o, 0, 1).reshape(B, T, ATT_Q)


def wkv_scan(r, w, k, v, kk, a, s0, reverse):
    xs = tuple(jnp.moveaxis(t, 1, 0) for t in (r, w, k, v, kk, a))

    def step(S, inp):
        r_t, w_t, k_t, v_t, kk_t, a_t = inp
        sa = jnp.einsum('bhvk,bhk->bhv', S, -kk_t)
        S = (S * w_t[:, :, None, :] + sa[..., None] * (kk_t * a_t)[:, :, None, :]
             + v_t[..., None] * k_t[:, :, None, :])
        return S, jnp.einsum('bhvk,bhk->bhv', S, r_t)

    sT, ys = lax.scan(step, s0, xs, reverse=reverse)
    return jnp.moveaxis(ys, 0, 1), sT


def rwkv_mix(h, r, k, v, lp, s0):
    f32 = jnp.float32
    B, T = h.shape[:2]
    hf = h.astype(f32)

    def heads(t):
        return t.astype(f32).reshape(B, T, R_HEADS, R_HEAD)

    r_h, k_h, v_h = heads(r), heads(k), heads(v)
    kk = k_h * lp['k_k'].astype(f32).reshape(R_HEADS, R_HEAD)
    kk = kk * lax.rsqrt(jnp.sum(kk * kk, -1, keepdims=True) + 1e-12)
    k_a = lp['k_a'].astype(f32).reshape(R_HEADS, R_HEAD)
    r_k = lp['r_k'].astype(f32)
    ys, bonuses, states = [], [], []
    for d in range(2):
        w_pre = lp['w0'][d] + jnp.tanh(hf @ lp['w1'][d]) @ lp['w2'][d]
        decay = jnp.exp(-jnp.exp(-jax.nn.softplus(-w_pre) - 0.5))
        a = jax.nn.sigmoid(lp['a0'][d] + (hf @ lp['a1'][d]) @ lp['a2'][d])
        decay_h, a_h = heads(decay), heads(a)
        k_d = k_h * (1.0 + (a_h - 1.0) * k_a)
        y_d, s_d = wkv_scan(r_h, decay_h, k_d, v_h, kk, a_h, s0[:, d].astype(f32), reverse=(d == 1))
        ys.append(y_d)
        bonuses.append(jnp.sum(r_h * k_d * r_k, -1, keepdims=True) * v_h)
        states.append(s_d)
    y = ys[0] + ys[1]
    mu = jnp.mean(y, -1, keepdims=True)
    var = jnp.mean(jnp.square(y - mu), -1, keepdims=True)
    y = ((y - mu) * lax.rsqrt(var + GN_EPS)).reshape(B, T, RW)
    y = y * lp['ln_g'] + lp['ln_b'] + (bonuses[0] + bonuses[1]).reshape(B, T, RW)
    g = jax.nn.sigmoid(hf @ lp['g1']) @ lp['g2']
    return (y * g).astype(h.dtype), jnp.stack(states, 1)


def project(h, lp):
    B, T = h.shape[:2]
    u = h @ lp['w_in']
    q, k_att, v_att, rkv = jnp.split(u, [ATT_Q, ATT_Q + ATT_KV, ATT_Q + 2 * ATT_KV], axis=-1)
    rkv = centred_shift(rkv, lp['mu'])
    r, k_rw, v_rw = jnp.split(rkv, 3, axis=-1)
    return (q.reshape(B, T, N_HEADS, HEAD_DIM), k_att.reshape(B, T, KV_HEADS, HEAD_DIM),
            v_att.reshape(B, T, KV_HEADS, HEAD_DIM), r, k_rw, v_rw)


def mix_context(h, lp):
    B = h.shape[0]
    q, k_att, v_att, r, k_rw, v_rw = project(h, lp)
    sink = lp['sink'].astype(jnp.float32).reshape(KV_HEADS, Q_GROUP)
    o_att = attn_context(q, k_att, v_att, sink)
    s0 = jnp.zeros((B, 2, R_HEADS, R_HEAD, R_HEAD), jnp.float32)
    o_rw, s_ctx = rwkv_mix(h, r, k_rw, v_rw, lp, s0)
    out = jnp.concatenate([o_att, o_rw], -1) @ lp['w_o']
    return out, k_att, v_att, s_ctx


def mix_latent(h, lp, ck, cv, s_ctx, row, col):
    q, k_att, v_att, r, k_rw, v_rw = project(h, lp)
    q = axial_rope(q, row, col)
    k_att = axial_rope(k_att, row, col)
    sink = lp['sink'].astype(jnp.float32).reshape(KV_HEADS, Q_GROUP)
    o_att = attn_latent(q, k_att, v_att, ck, cv, sink)
    o_rw, _ = rwkv_mix(h, r, k_rw, v_rw, lp, s_ctx)
    return jnp.concatenate([o_att, o_rw], -1) @ lp['w_o']


def swiglu(h, w1, w3, w2):
    return (jax.nn.silu(h @ w1) * (h @ w3)) @ w2


def moe_swiglu(h, router, w1, w3, w2):
    probs = jax.nn.softmax((h @ router).astype(jnp.float32), axis=-1)
    top_p, top_i = lax.top_k(probs, TOP_K)
    top_p = top_p / jnp.sum(top_p, -1, keepdims=True)
    gates = jnp.sum(jax.nn.one_hot(top_i, N_EXPERTS, dtype=jnp.float32) * top_p[..., None], axis=-2)
    y = jnp.zeros(h.shape, jnp.float32)
    for e in range(N_EXPERTS):
        y = y + gates[..., e:e + 1] * swiglu(h, w1[e], w3[e], w2[e]).astype(jnp.float32)
    return y.astype(h.dtype)


def channel_mix(h, l, ffn_w1, ffn_w3, ffn_w2, moe_router, moe_w1, moe_w3, moe_w2):
    if l % 2 == 0:
        i = l // 2
        return swiglu(h, ffn_w1[i], ffn_w3[i], ffn_w2[i])
    i = l // 2
    return moe_swiglu(h, moe_router[i], moe_w1[i], moe_w3[i], moe_w2[i])


def setup_inputs(seed: int = 0) -> dict:
    key = jax.random.key(seed)
    ks = iter(jax.random.split(key, 48))

    def nrm(shape, scale):
        return jax.random.normal(next(ks), shape, jnp.float32) * scale

    D = D_MODEL
    return {
        'x_prompt': nrm((BATCH, SEQ, D), 1.0),
        'x_sample': nrm((DEC_BATCH, DEC_SEQ, D), 1.0),
        'cache_k': nrm((DEC_BATCH, DEPTH, PAST_LEN, KV_HEADS, HEAD_DIM), 1.0),
        'cache_v': nrm((DEC_BATCH, DEPTH, PAST_LEN, KV_HEADS, HEAD_DIM), 1.0),
        'state_rwkv': nrm((DEC_BATCH, DEPTH, 2, R_HEADS, R_HEAD, R_HEAD), 0.5),
        'c': nrm((DEC_BATCH, D), 1.0),
        'c_ctx': nrm((D,), 1.0),
        'w_mod': nrm((DEPTH, D, 6 * D), 0.5 * D ** -0.5),
        'b_mod': nrm((DEPTH, 6 * D), 0.02),
        'g_pre_mix': 1.0 + nrm((DEPTH, D), 0.1),
        'g_post_mix': 1.0 + nrm((DEPTH, D), 0.1),
        'g_pre_ffn': 1.0 + nrm((DEPTH, D), 0.1),
        'g_post_ffn': 1.0 + nrm((DEPTH, D), 0.1),
        'w_in': nrm((DEPTH, D, D_IN), D ** -0.5),
        'mu_shift': 0.5 + nrm((DEPTH, 3 * RW), 0.1),
        'w_o': nrm((DEPTH, MIX_W, D), MIX_W ** -0.5),
        'attn_sink': nrm((DEPTH, N_HEADS), 0.5),
        'rw_w0': -0.5 + nrm((DEPTH, 2, RW), 0.5),
        'rw_w1': nrm((DEPTH, 2, D, LORA_W), D ** -0.5),
        'rw_w2': nrm((DEPTH, 2, LORA_W, RW), 0.1 * LORA_W ** -0.5),
        'rw_a0': nrm((DEPTH, 2, RW), 0.5),
        'rw_a1': nrm((DEPTH, 2, D, LORA_A), D ** -0.5),
        'rw_a2': nrm((DEPTH, 2, LORA_A, RW), 0.5 * LORA_A ** -0.5),
        'rw_g1': nrm((DEPTH, D, LORA_G), D ** -0.5),
        'rw_g2': nrm((DEPTH, LORA_G, RW), LORA_G ** -0.5),
        'rw_k_k': 0.85 + nrm((DEPTH, RW), 0.05),
        'rw_k_a': 1.0 + nrm((DEPTH, RW), 0.05),
        'rw_r_k': nrm((DEPTH, R_HEADS, R_HEAD), 0.1),
        'rw_ln_g': 1.0 + nrm((DEPTH, RW), 0.1),
        'rw_ln_b': nrm((DEPTH, RW), 0.02),
        'ffn_w1': nrm((N_DENSE, D, D_FF), D ** -0.5),
        'ffn_w3': nrm((N_DENSE, D, D_FF), D ** -0.5),
        'ffn_w2': nrm((N_DENSE, D_FF, D), D_FF ** -0.5),
        'moe_router': nrm((N_MOE, D, N_EXPERTS), D ** -0.5),
        'moe_w1': nrm((N_MOE, N_EXPERTS, D, D_FF_EXPERT), D ** -0.5),
        'moe_w3': nrm((N_MOE, N_EXPERTS, D, D_FF_EXPERT), D ** -0.5),
        'moe_w2': nrm((N_MOE, N_EXPERTS, D_FF_EXPERT, D), D_FF_EXPERT ** -0.5),
    }


def reference(x_prompt, x_sample, cache_k, cache_v, state_rwkv, c, c_ctx,
              w_mod, b_mod, g_pre_mix, g_post_mix, g_pre_ffn, g_post_ffn,
              w_in, mu_shift, w_o, attn_sink,
              rw_w0, rw_w1, rw_w2, rw_a0, rw_a1, rw_a2, rw_g1, rw_g2,
              rw_k_k, rw_k_a, rw_r_k, rw_ln_g, rw_ln_b,
              ffn_w1, ffn_w3, ffn_w2, moe_router, moe_w1, moe_w3, moe_w2):
    T = x_sample.shape[1]
    n_rows = T // GRID_W
    row = jnp.repeat(jnp.arange(n_rows), GRID_W)
    col = jnp.tile(jnp.arange(GRID_W), n_rows)
    cond_ctx = c_ctx[None, None, :]
    cond_lat = c[:, None, :]
    xp, xs = x_prompt, x_sample
    new_k, new_v, new_s = [], [], []
    for l in range(DEPTH):
        lp = {'w_in': w_in[l], 'mu': mu_shift[l], 'w_o': w_o[l], 'sink': attn_sink[l],
              'w0': rw_w0[l], 'w1': rw_w1[l], 'w2': rw_w2[l],
              'a0': rw_a0[l], 'a1': rw_a1[l], 'a2': rw_a2[l],
              'g1': rw_g1[l], 'g2': rw_g2[l], 'k_k': rw_k_k[l], 'k_a': rw_k_a[l],
              'r_k': rw_r_k[l], 'ln_g': rw_ln_g[l], 'ln_b': rw_ln_b[l]}
        sh1, sc1, gt1, sh2, sc2, gt2 = modulation(cond_ctx, w_mod[l], b_mod[l])
        h = rmsnorm(xp, g_pre_mix[l]) * (1.0 + sc1) + sh1
        o, k_ctx, v_ctx, s_ctx = mix_context(h, lp)
        xp = xp + gt1 * rmsnorm(o, g_post_mix[l])
        h = rmsnorm(xp, g_pre_ffn[l]) * (1.0 + sc2) + sh2
        f = channel_mix(h, l, ffn_w1, ffn_w3, ffn_w2, moe_router, moe_w1, moe_w3, moe_w2)
        xp = xp + gt2 * rmsnorm(f, g_post_ffn[l])
        new_k.append(k_ctx)
        new_v.append(v_ctx)
        new_s.append(s_ctx.astype(state_rwkv.dtype))
        sh1, sc1, gt1, sh2, sc2, gt2 = modulation(cond_lat, w_mod[l], b_mod[l])
        h = rmsnorm(xs, g_pre_mix[l]) * (1.0 + sc1) + sh1
        o = mix_latent(h, lp, cache_k[:, l], cache_v[:, l], state_rwkv[:, l], row, col)
        xs = xs + gt1 * rmsnorm(o, g_post_mix[l])
        h = rmsnorm(xs, g_pre_ffn[l]) * (1.0 + sc2) + sh2
        f = channel_mix(h, l, ffn_w1, ffn_w3, ffn_w2, moe_router, moe_w1, moe_w3, moe_w2)
        xs = xs + gt2 * rmsnorm(f, g_post_ffn[l])
    new_cache_k = jnp.stack(new_k, axis=1)
    new_cache_v = jnp.stack(new_v, axis=1)
    new_state_rwkv = jnp.stack(new_s, axis=1)
    return (xp, xs, new_cache_k, new_cache_v, new_state_rwkv)
```

```python
import functools

import jax
import jax.numpy as jnp
from jax import lax
from jax.experimental import pallas as pl
from jax.experimental.pallas import tpu as pltpu

F32 = jnp.float32
BF16 = jnp.bfloat16

D_MODEL = 1024
DEPTH = 4
GRID_W = 64
N_HEADS = 8
KV_HEADS = 2
HEAD_DIM = 64
WINDOW = 128
ROPE_THETA = 10000.0
R_HEADS = 8
R_HEAD = 64
RW = R_HEADS * R_HEAD
LORA_W = 64
LORA_A = 64
LORA_G = 128
LORA_ALL = 2 * LORA_W + 2 * LORA_A + LORA_G
ATT_Q = N_HEADS * HEAD_DIM
ATT_KV = KV_HEADS * HEAD_DIM
D_IN = ATT_Q + 2 * ATT_KV + 3 * RW
D_FF = 2816
N_EXPERTS = 8
D_FF_EXPERT = 1408
NORM_EPS = 1e-6
GN_EPS = 64e-5
NEG_INF = -1e30

TILE = 256
CHUNK = 64
PAIR = 2 * R_HEAD
N_PAIRS = R_HEADS // 2
FFN_TILE = 512
FF_BLOCK = 1408
MOD_COLS = 1536
VMEM_LIMIT = 48 * 1024 * 1024

_NT = (((1,), (1,)), ((), ()))


def _params(n_parallel, n_arbitrary=0):
    sem = ("parallel",) * n_parallel + ("arbitrary",) * n_arbitrary
    return pltpu.CompilerParams(dimension_semantics=sem, vmem_limit_bytes=VMEM_LIMIT)


def _sigmoid(x):
    return 1.0 / (1.0 + jnp.exp(-x))


def _mm(a, b):
    return jnp.dot(a.astype(BF16), b.astype(BF16), preferred_element_type=F32)


def _mm_nt(a, b):
    return lax.dot_general(a.astype(BF16), b.astype(BF16), _NT, preferred_element_type=F32)


def _split3(x):
    x0 = x.astype(BF16)
    r1 = x - x0.astype(F32)
    x1 = r1.astype(BF16)
    x2 = (r1 - x1.astype(F32)).astype(BF16)
    return x0, x1, x2


def _mm_ones_right(x, ones_bf16):
    x0, x1, x2 = _split3(x)
    d = functools.partial(jnp.dot, preferred_element_type=F32)
    return d(x0, ones_bf16) + d(x1, ones_bf16) + d(x2, ones_bf16)


def _mm_ones_left(ones_bf16, x):
    x0, x1, x2 = _split3(x)
    d = functools.partial(jnp.dot, preferred_element_type=F32)
    return d(ones_bf16, x0) + d(ones_bf16, x1) + d(ones_bf16, x2)


def _rms(x):
    return x * lax.rsqrt(jnp.mean(x * x, -1, keepdims=True) + NORM_EPS)


def _mod_row(i, n_ctx_tiles, tiles_per_lat):
    return jnp.where(i < n_ctx_tiles, 0, 1 + (i - n_ctx_tiles) // tiles_per_lat)


def _mod_kernel(cond_ref, w_ref, b_ref, o_ref):
    c = cond_ref[...]
    s = c * _sigmoid(c)
    o_ref[0] = _mm(s, w_ref[0]) + b_ref[0]


def _modulation(cond, w_mod, b_mod):
    n_cond = cond.shape[0]
    n_col = 6 * D_MODEL // MOD_COLS
    return pl.pallas_call(
        _mod_kernel,
        grid=(DEPTH, n_col),
        in_specs=[pl.BlockSpec((n_cond, D_MODEL), lambda l, j: (0, 0)),
                  pl.BlockSpec((1, D_MODEL, MOD_COLS), lambda l, j: (l, 0, j)),
                  pl.BlockSpec((1, 1, MOD_COLS), lambda l, j: (l, 0, j))],
        out_specs=pl.BlockSpec((1, n_cond, MOD_COLS), lambda l, j: (l, 0, j)),
        out_shape=jax.ShapeDtypeStruct((DEPTH, n_cond, 6 * D_MODEL), F32),
        compiler_params=_params(2),
        name="modulation",
    )(cond, w_mod, b_mod.reshape(DEPTH, 1, 6 * D_MODEL))


def _inproj_kernel(x_ref, mod_ref, g_ref, w_ref, q_ref, kv_ref, rkv_ref, lora_ref):
    m = mod_ref[0]
    h = _rms(x_ref[...]) * g_ref[...]
    h = h * (1.0 + m[1:2]) + m[0:1]
    u = jnp.dot(h.astype(BF16), w_ref[...], preferred_element_type=F32)
    q_ref[...] = u[:, :ATT_Q]
    kv_ref[...] = u[:, ATT_Q:ATT_Q + 2 * ATT_KV]
    rkv_ref[...] = u[:, ATT_Q + 2 * ATT_KV:D_IN]
    lora_ref[...] = u[:, D_IN:]


def _inproj(x, mod_l, g, w_cat, mod_map):
    nt = x.shape[0]
    row = lambda i: (i, 0)
    const = lambda i: (0, 0)
    return pl.pallas_call(
        _inproj_kernel,
        grid=(nt // TILE,),
        in_specs=[pl.BlockSpec((TILE, D_MODEL), row),
                  pl.BlockSpec((1, 6, D_MODEL), lambda i: (mod_map(i), 0, 0)),
                  pl.BlockSpec((1, D_MODEL), const),
                  pl.BlockSpec((D_MODEL, D_IN + LORA_ALL), const)],
        out_specs=[pl.BlockSpec((TILE, ATT_Q), row),
                   pl.BlockSpec((TILE, 2 * ATT_KV), row),
                   pl.BlockSpec((TILE, 3 * RW), row),
                   pl.BlockSpec((TILE, LORA_ALL), row)],
        out_shape=[jax.ShapeDtypeStruct((nt, ATT_Q), F32),
                   jax.ShapeDtypeStruct((nt, 2 * ATT_KV), F32),
                   jax.ShapeDtypeStruct((nt, 3 * RW), F32),
                   jax.ShapeDtypeStruct((nt, LORA_ALL), F32)],
        compiler_params=_params(1),
        name="inproj",
    )(x, mod_l, g.reshape(1, D_MODEL), w_cat)


def _prep_kernel(rkv_ref, prev_ref, next_ref, lora_ref, mu_ref, w0_ref, w2_ref, a0_ref, a2_ref,
                 g2_ref, kk_ref, ka_ref, rk_ref, bd_ref, tri_ref,
                 v_out, al_out, be_out, ka_out, rh_out, wl_out, bonus_out, g_out,
                 *, n_ctx_tiles, tiles_per_lat):
    i = pl.program_id(0)
    j = (i - n_ctx_tiles) % tiles_per_lat
    is_ctx = i < n_ctx_tiles
    first = is_ctx | (j == 0)
    last = is_ctx | (j == tiles_per_lat - 1)

    u = rkv_ref[...]
    row = lax.broadcasted_iota(jnp.int32, (TILE, 1), 0)
    p_row = prev_ref[7:8, :] * (1.0 - first.astype(F32))
    n_row = next_ref[0:1, :] * (1.0 - last.astype(F32))
    prev = jnp.where(row == 0, p_row, pltpu.roll(u, 1, 0))
    nxt = jnp.where(row == TILE - 1, n_row, pltpu.roll(u, TILE - 1, 0))
    us = u + mu_ref[...] * (0.5 * (prev + nxt) - u)
    r = us[:, :RW]
    k = us[:, RW:2 * RW]
    v = us[:, 2 * RW:]

    bd = bd_ref[...]
    kk = k * kk_ref[...]
    kk = kk * lax.rsqrt(_mm_ones_right(kk * kk, bd) + 1e-12)

    lora = lora_ref[...]
    lw_in = jnp.tanh(lora[:, :2 * LORA_W])
    la_in = lora[:, 2 * LORA_W:2 * LORA_W + 2 * LORA_A]
    bonus = jnp.zeros_like(v)
    for d in range(2):
        w_pre = w0_ref[d:d + 1, :] + _mm(lw_in, w2_ref[d])
        lw = -jnp.exp(-0.5) * _sigmoid(w_pre)
        a = _sigmoid(a0_ref[d:d + 1, :] + _mm(la_in, a2_ref[d]))
        kd = k * (1.0 + (a - 1.0) * ka_ref[...])
        bonus = bonus + _mm_ones_right(r * kd * rk_ref[...], bd) * v
        cw = _mm_ones_left(tri_ref[d], lw)
        e_neg = jnp.exp(-cw)
        al_out[d] = (kk * jnp.exp(cw - lw)).astype(BF16)
        be_out[d] = (kk * a * e_neg).astype(BF16)
        ka_out[d] = (kd * e_neg).astype(BF16)
        rh_out[d] = (r * jnp.exp(cw)).astype(BF16)
        for c in range(TILE // CHUNK):
            end = c * CHUNK + (CHUNK - 1 if d == 0 else 0)
            wl_out[0, 4 * d + c:4 * d + c + 1, :] = jnp.exp(cw[end:end + 1, :])
    v_out[...] = v.astype(BF16)
    bonus_out[...] = bonus
    g_out[...] = _mm(_sigmoid(lora[:, 2 * LORA_W + 2 * LORA_A:]), g2_ref[...])


def _prep(rkv, lora, p, consts, n_ctx_tiles, tiles_per_lat):
    nt = rkv.shape[0]
    n_tiles = nt // TILE
    row = lambda i: (i, 0)
    drow = lambda i: (0, i, 0)
    c2 = lambda i: (0, 0)
    c3 = lambda i: (0, 0, 0)
    hb = TILE // 8
    kern = functools.partial(_prep_kernel, n_ctx_tiles=n_ctx_tiles, tiles_per_lat=tiles_per_lat)
    dshape = jax.ShapeDtypeStruct((2, nt, RW), BF16)
    return pl.pallas_call(
        kern,
        grid=(n_tiles,),
        in_specs=[pl.BlockSpec((TILE, 3 * RW), row),
                  pl.BlockSpec((8, 3 * RW), lambda i: (jnp.maximum(i * hb - 1, 0), 0)),
                  pl.BlockSpec((8, 3 * RW), lambda i: (jnp.minimum((i + 1) * hb, nt // 8 - 1), 0)),
                  pl.BlockSpec((TILE, LORA_ALL), row),
                  pl.BlockSpec((1, 3 * RW), c2),
                  pl.BlockSpec((2, RW), c2),
                  pl.BlockSpec((2, 2 * LORA_W, RW), c3),
                  pl.BlockSpec((2, RW), c2),
                  pl.BlockSpec((2, 2 * LORA_A, RW), c3),
                  pl.BlockSpec((LORA_G, RW), c2),
                  pl.BlockSpec((1, RW), c2),
                  pl.BlockSpec((1, RW), c2),
                  pl.BlockSpec((1, RW), c2),
                  pl.BlockSpec((RW, RW), c2),
                  pl.BlockSpec((2, TILE, TILE), c3)],
        out_specs=[pl.BlockSpec((TILE, RW), row),
                   pl.BlockSpec((2, TILE, RW), drow),
                   pl.BlockSpec((2, TILE, RW), drow),
                   pl.BlockSpec((2, TILE, RW), drow),
                   pl.BlockSpec((2, TILE, RW), drow),
                   pl.BlockSpec((1, 8, RW), lambda i: (i, 0, 0)),
                   pl.BlockSpec((TILE, RW), row),
                   pl.BlockSpec((TILE, RW), row)],
        out_shape=[jax.ShapeDtypeStruct((nt, RW), BF16), dshape, dshape, dshape, dshape,
                   jax.ShapeDtypeStruct((n_tiles, 8, RW), F32),
                   jax.ShapeDtypeStruct((nt, RW), F32),
                   jax.ShapeDtypeStruct((nt, RW), F32)],
        compiler_params=_params(1),
        name="rwkv_prep",
    )(rkv, rkv, rkv, lora, p["mu"], p["w0"], p["w2"], p["a0"], p["a2"], p["g2"],
      p["k_k"], p["k_a"], p["r_k"], consts["bd"], consts["tri"])


def _scan_kernel(al_ref, be_ref, ka_ref, rh_ref, v_ref, wl_ref, s0_ref, y_ref, st_ref, s_scr,
                 *, n_chunks):
    d = pl.program_id(1)
    lane = lax.broadcasted_iota(jnp.int32, (CHUNK, PAIR), 1)
    lo = lane < R_HEAD
    ri = lax.broadcasted_iota(jnp.int32, (PAIR, PAIR), 0)
    ci = lax.broadcasted_iota(jnp.int32, (PAIR, PAIR), 1)
    same = (ri // CHUNK) == (ci // CHUNK)
    rt = ri % CHUNK
    ct = ci % CHUNK
    ahead = (rt - ct) * (1 - 2 * d)
    before = (ahead > 0) & same
    upto = (ahead >= 0) & same

    s_scr[...] = s0_ref[0, 0, 0]

    def expand(ref, off):
        x = ref[0, pl.ds(off, CHUNK), :]
        zero = jnp.zeros_like(x)
        return jnp.concatenate([jnp.where(lo, x, zero), jnp.where(lo, zero, x)], 0)

    def chunk(c, carry):
        ce = jnp.where(d == 0, c, n_chunks - 1 - c)
        off = pl.multiple_of(ce * CHUNK, CHUNK)
        a_e = expand(al_ref, off)
        b_e = expand(be_ref, off)
        k_e = expand(ka_ref, off)
        r_e = expand(rh_ref, off)
        vx = v_ref[pl.ds(off, CHUNK), :]
        vzero = jnp.zeros_like(vx)
        v_e = jnp.concatenate([jnp.where(lo, vx, vzero), jnp.where(lo, vzero, vx)], 0)

        s = s_scr[...]
        s_b = s.astype(BF16)
        zero = jnp.zeros((PAIR, PAIR), F32)
        m_ab = jnp.where(before, _mm_nt(a_e, b_e), zero)
        m_ak = jnp.where(before, _mm_nt(a_e, k_e), zero)
        m_rb = jnp.where(upto, _mm_nt(r_e, b_e), zero)
        m_rk = jnp.where(upto, _mm_nt(r_e, k_e), zero)

        x = _mm_nt(a_e, s_b) + _mm(m_ak, v_e)
        p = m_ab
        x = x - _mm(p, x)
        n_sq = CHUNK.bit_length() - 2
        for _ in range(n_sq):
            p = _mm(p, p)
            x = x + _mm(p, x)
        z = -x

        y2 = _mm_nt(r_e, s_b) + _mm(m_rb, z) + _mm(m_rk, v_e)
        y_ref[0, pl.ds(off, CHUNK), :] = y2[:CHUNK] + y2[CHUNK:]

        zv_t = jnp.concatenate([z, v_e.astype(F32)], 0).T
        bk = jnp.concatenate([b_e, k_e], 0)
        tile_i = ce // (TILE // CHUNK)
        wrow = 4 * d + ce % (TILE // CHUNK)
        wl = wl_ref[tile_i, pl.ds(wrow, 1), :]
        s_scr[...] = (s + _mm(zv_t, bk)) * wl
        return carry

    lax.fori_loop(0, n_chunks, chunk, 0)
    st_ref[0, 0, 0] = s_scr[...]


def _expand_state(s0):
    n = s0.shape[0]
    s = s0.reshape(n, 2, N_PAIRS, 2, R_HEAD, R_HEAD)
    z = jnp.zeros_like(s[:, :, :, 0])
    top = jnp.concatenate([s[:, :, :, 0], z], -1)
    bot = jnp.concatenate([z, s[:, :, :, 1]], -1)
    return jnp.concatenate([top, bot], -2)


def _collapse_state(st):
    n = st.shape[0]
    h0 = st[:, :, :, :R_HEAD, :R_HEAD]
    h1 = st[:, :, :, R_HEAD:, R_HEAD:]
    return jnp.stack([h0, h1], 3).reshape(n, 2, R_HEADS, R_HEAD, R_HEAD)


def _scan(al, be, ka, rh, v, wl, s0, row0, n_seq, seq_len):
    blk0 = row0 // seq_len
    tiles = seq_len // TILE
    tile0 = row0 // TILE
    dmap = lambda b, d, p: (d, blk0 + b, p)
    kern = functools.partial(_scan_kernel, n_chunks=seq_len // CHUNK)
    y, st = pl.pallas_call(
        kern,
        grid=(n_seq, 2, N_PAIRS),
        in_specs=[pl.BlockSpec((1, seq_len, PAIR), dmap),
                  pl.BlockSpec((1, seq_len, PAIR), dmap),
                  pl.BlockSpec((1, seq_len, PAIR), dmap),
                  pl.BlockSpec((1, seq_len, PAIR), dmap),
                  pl.BlockSpec((seq_len, PAIR), lambda b, d, p: (blk0 + b, p)),
                  pl.BlockSpec((tiles, 8, PAIR), lambda b, d, p: (tile0 // tiles + b, 0, p)),
                  pl.BlockSpec((1, 1, 1, PAIR, PAIR), lambda b, d, p: (b, d, p, 0, 0))],
        out_specs=[pl.BlockSpec((1, seq_len, PAIR), lambda b, d, p: (d, b, p)),
                   pl.BlockSpec((1, 1, 1, PAIR, PAIR), lambda b, d, p: (b, d, p, 0, 0))],
        out_shape=[jax.ShapeDtypeStruct((2, n_seq * seq_len, RW), F32),
                   jax.ShapeDtypeStruct((n_seq, 2, N_PAIRS, PAIR, PAIR), F32)],
        scratch_shapes=[pltpu.VMEM((PAIR, PAIR), F32)],
        compiler_params=_params(3),
        name="wkv_scan",
    )(al, be, ka, rh, v, wl, _expand_state(s0))
    return y, _collapse_state(st)


def _dup_halves(x, lo):
    xr = pltpu.roll(x, HEAD_DIM, 1)
    return jnp.where(lo, x, xr), jnp.where(lo, xr, x)


def _stack_heads(qp, lo):
    zero = jnp.zeros_like(qp)
    return jnp.concatenate([jnp.where(lo, qp, zero), jnp.where(lo, zero, qp)], 0)


def _sink_col(sink_ref, p, rows):
    r = lax.broadcasted_iota(jnp.int32, (2 * rows, 1), 0)
    return jnp.where(r < rows, sink_ref[2 * p:2 * p + 1, 0:1], sink_ref[2 * p + 1:2 * p + 2, 0:1])


def _attn_ctx_kernel(q_ref, kv_ref, sink_ref, o_ref):
    rows = q_ref.shape[0]
    lo = lax.broadcasted_iota(jnp.int32, (rows, PAIR), 1) < HEAD_DIM
    k_dup = _dup_halves(kv_ref[:, :ATT_KV], lo)
    v_dup = _dup_halves(kv_ref[:, ATT_KV:], lo)
    scale = HEAD_DIM ** -0.5
    for p in range(N_HEADS // 2):
        g = (2 * p) // (N_HEADS // KV_HEADS)
        qs = _stack_heads(q_ref[:, PAIR * p:PAIR * (p + 1)] * scale, lo)
        s = _mm_nt(qs, k_dup[g])
        sk = _sink_col(sink_ref, p, rows)
        m = jnp.maximum(jnp.max(s, -1, keepdims=True), sk)
        e = jnp.exp(s - m)
        den = jnp.sum(e, -1, keepdims=True) + jnp.exp(sk - m)
        o2 = _mm(e / den, v_dup[g])
        o_ref[:, PAIR * p:PAIR * (p + 1)] = jnp.where(lo, o2[:rows], o2[rows:])


def _attn_ctx(q, kv, sink_rows, n_seq, seq_len):
    return pl.pallas_call(
        _attn_ctx_kernel,
        grid=(n_seq,),
        in_specs=[pl.BlockSpec((seq_len, ATT_Q), lambda b: (b, 0)),
                  pl.BlockSpec((seq_len, 2 * ATT_KV), lambda b: (b, 0)),
                  pl.BlockSpec((N_HEADS, 128), lambda b: (0, 0))],
        out_specs=pl.BlockSpec((seq_len, ATT_Q), lambda b: (b, 0)),
        out_shape=jax.ShapeDtypeStruct((n_seq * seq_len, ATT_Q), F32),
        compiler_params=_params(1),
        name="attn_ctx",
    )(q, kv, sink_rows)


def _rope(x, cos, sgn_sin, first16):
    sw = jnp.where(first16, pltpu.roll(x, PAIR - 16, 1), pltpu.roll(x, 16, 1))
    return x * cos + sw * sgn_sin


def _attn_lat_kernel(q_ref, kv_ref, ck_ref, cv_ref, cos_ref, sin_ref, sink_ref, o_ref,
                     kd_scr, vd_scr, ckd_scr, cvd_scr, *, seq_len):
    i = pl.program_id(1)
    qb = q_ref.shape[0]
    n_blk = seq_len // qb

    @pl.when(i == 0)
    def _():
        lo_s = lax.broadcasted_iota(jnp.int32, (seq_len, PAIR), 1) < HEAD_DIM
        f16_s = (lax.broadcasted_iota(jnp.int32, (seq_len, PAIR), 1) % 32) < 16
        k_r = _rope(kv_ref[:, :ATT_KV], cos_ref[...], sin_ref[...], f16_s)
        k0, k1 = _dup_halves(k_r, lo_s)
        v0, v1 = _dup_halves(kv_ref[:, ATT_KV:], lo_s)
        kd_scr[0] = k0.astype(BF16)
        kd_scr[1] = k1.astype(BF16)
        vd_scr[0] = v0.astype(BF16)
        vd_scr[1] = v1.astype(BF16)
        lo_c = lax.broadcasted_iota(jnp.int32, ck_ref.shape[1:], 1) < HEAD_DIM
        c0, c1 = _dup_halves(ck_ref[0], lo_c)
        ckd_scr[0] = c0.astype(BF16)
        ckd_scr[1] = c1.astype(BF16)
        c0, c1 = _dup_halves(cv_ref[0], lo_c)
        cvd_scr[0] = c0.astype(BF16)
        cvd_scr[1] = c1.astype(BF16)

    lo = lax.broadcasted_iota(jnp.int32, (qb, PAIR), 1) < HEAD_DIM
    f16 = (lax.broadcasted_iota(jnp.int32, (qb, PAIR), 1) % 32) < 16
    q0 = pl.multiple_of(i * qb, qb)
    cos = cos_ref[pl.ds(q0, qb), :]
    sin = sin_ref[pl.ds(q0, qb), :]
    qpos = i * qb + lax.broadcasted_iota(jnp.int32, (2 * qb, qb), 0) % qb
    kcol = lax.broadcasted_iota(jnp.int32, (2 * qb, qb), 1)
    scale = HEAD_DIM ** -0.5
    n_past = ck_ref.shape[1]

    for p in range(N_HEADS // 2):
        g = (2 * p) // (N_HEADS // KV_HEADS)
        qp = _rope(q_ref[:, PAIR * p:PAIR * (p + 1)], cos, sin, f16) * scale
        qs = _stack_heads(qp, lo).astype(BF16)
        parts = [_mm_nt(qs, ckd_scr[g])]
        vals = [cvd_scr[g]]
        for jj in range(3):
            j = i - 1 + jj
            jc = jnp.clip(j, 0, n_blk - 1)
            k0 = pl.multiple_of(jc * qb, qb)
            kpos = jnp.where(j == jc, jc * qb, -4 * seq_len) + kcol
            ok = jnp.abs(qpos - kpos) <= WINDOW
            sw = _mm_nt(qs, kd_scr[g, pl.ds(k0, qb), :])
            parts.append(jnp.where(ok, sw, NEG_INF))
            vals.append(vd_scr[g, pl.ds(k0, qb), :])
        s = jnp.concatenate(parts, 1)
        sk = _sink_col(sink_ref, p, qb)
        m = jnp.maximum(jnp.max(s, -1, keepdims=True), sk)
        e = jnp.exp(s - m)
        den = jnp.sum(e, -1, keepdims=True) + jnp.exp(sk - m)
        pr = (e / den).astype(BF16)
        o2 = jnp.dot(pr[:, :n_past], vals[0], preferred_element_type=F32)
        for jj in range(3):
            c0 = n_past + jj * qb
            o2 = o2 + jnp.dot(pr[:, c0:c0 + qb], vals[1 + jj], preferred_element_type=F32)
        o_ref[:, PAIR * p:PAIR * (p + 1)] = jnp.where(lo, o2[:qb], o2[qb:])


def _attn_lat(q, kv, ck, cv, cos_t, sin_t, sink_rows, row0, n_seq, seq_len):
    qb = WINDOW
    n_blk = seq_len // qb
    blk0 = row0 // qb
    seq0 = row0 // seq_len
    n_past = ck.shape[1]
    kern = functools.partial(_attn_lat_kernel, seq_len=seq_len)
    return pl.pallas_call(
        kern,
        grid=(n_seq, n_blk),
        in_specs=[pl.BlockSpec((qb, ATT_Q), lambda b, i: (blk0 + b * n_blk + i, 0)),
                  pl.BlockSpec((seq_len, 2 * ATT_KV), lambda b, i: (seq0 + b, 0)),
                  pl.BlockSpec((1, n_past, ATT_KV), lambda b, i: (b, 0, 0)),
                  pl.BlockSpec((1, n_past, ATT_KV), lambda b, i: (b, 0, 0)),
                  pl.BlockSpec((seq_len, PAIR), lambda b, i: (0, 0)),
                  pl.BlockSpec((seq_len, PAIR), lambda b, i: (0, 0)),
                  pl.BlockSpec((N_HEADS, 128), lambda b, i: (0, 0))],
        out_specs=pl.BlockSpec((qb, ATT_Q), lambda b, i: (b * n_blk + i, 0)),
        out_shape=jax.ShapeDtypeStruct((n_seq * seq_len, ATT_Q), F32),
        scratch_shapes=[pltpu.VMEM((2, seq_len, PAIR), BF16), pltpu.VMEM((2, seq_len, PAIR), BF16),
                        pltpu.VMEM((2, n_past, PAIR), BF16), pltpu.VMEM((2, n_past, PAIR), BF16)],
        compiler_params=_params(1, 1),
        name="attn_lat",
    )(q, kv, ck, cv, cos_t, sin_t, sink_rows)


def _postmix_kernel(x_ref, oatt_ref, y_ref, bonus_ref, g_ref, mod_ref, gpost_ref, gpre_ref,
                    lng_ref, lnb_ref, wo_ref, bd_ref, router_ref, x1_ref, h_ref, gates_ref,
                    *, with_router):
    m = mod_ref[0]
    bd = bd_ref[...]
    y = y_ref[0] + y_ref[1]
    inv_n = 1.0 / R_HEAD
    mu = _mm_ones_right(y, bd) * inv_n
    yc = y - mu
    var = _mm_ones_right(yc * yc, bd) * inv_n
    yn = yc * lax.rsqrt(var + GN_EPS)
    o_rw = (yn * lng_ref[...] + lnb_ref[...] + bonus_ref[...]) * g_ref[...]
    o = (jnp.dot(oatt_ref[...].astype(BF16), wo_ref[:ATT_Q, :], preferred_element_type=F32)
         + jnp.dot(o_rw.astype(BF16), wo_ref[ATT_Q:, :], preferred_element_type=F32))
    x1 = x_ref[...] + m[2:3] * (_rms(o) * gpost_ref[...])
    x1_ref[...] = x1
    h = (_rms(x1) * gpre_ref[...]) * (1.0 + m[4:5]) + m[3:4]
    h_ref[...] = h.astype(BF16)
    if with_router:
        logits = jnp.dot(h, router_ref[...], preferred_element_type=F32,
                         precision=lax.Precision.HIGHEST)
        lane = lax.broadcasted_iota(jnp.int32, logits.shape, 1)
        valid = lane < N_EXPERTS
        logits = jnp.where(valid, logits, NEG_INF)
        e = jnp.exp(logits - jnp.max(logits, -1, keepdims=True))
        probs = e / jnp.sum(e, -1, keepdims=True)
        lane_f = lane.astype(F32)
        p1 = jnp.max(probs, -1, keepdims=True)
        i1 = jnp.min(jnp.where(probs == p1, lane_f, 1e9), -1, keepdims=True)
        rest = jnp.where(lane_f == i1, -1.0, probs)
        p2 = jnp.max(rest, -1, keepdims=True)
        i2 = jnp.min(jnp.where(rest == p2, lane_f, 1e9), -1, keepdims=True)
        sel = (lane_f == i1) | (lane_f == i2)
        gates_ref[...] = jnp.where(sel, probs / (p1 + p2), 0.0)
    else:
        gates_ref[...] = jnp.ones(gates_ref.shape, F32)


def _postmix(x, o_att, y, bonus, g, mod_l, p, consts, mod_map, router_pad):
    nt = x.shape[0]
    row = lambda i: (i, 0)
    c2 = lambda i: (0, 0)
    with_router = router_pad is not None
    if router_pad is None:
        router_pad = jnp.zeros((D_MODEL, 128), F32)
    kern = functools.partial(_postmix_kernel, with_router=with_router)
    vec = lambda a: a.reshape(1, -1)
    return pl.pallas_call(
        kern,
        grid=(nt // TILE,),
        in_specs=[pl.BlockSpec((TILE, D_MODEL), row),
                  pl.BlockSpec((TILE, ATT_Q), row),
                  pl.BlockSpec((2, TILE, RW), lambda i: (0, i, 0)),
                  pl.BlockSpec((TILE, RW), row),
                  pl.BlockSpec((TILE, RW), row),
                  pl.BlockSpec((1, 6, D_MODEL), lambda i: (mod_map(i), 0, 0)),
                  pl.BlockSpec((1, D_MODEL), c2),
                  pl.BlockSpec((1, D_MODEL), c2),
                  pl.BlockSpec((1, RW), c2),
                  pl.BlockSpec((1, RW), c2),
                  pl.BlockSpec((ATT_Q + RW, D_MODEL), c2),
                  pl.BlockSpec((RW, RW), c2),
                  pl.BlockSpec((D_MODEL, 128), c2)],
        out_specs=[pl.BlockSpec((TILE, D_MODEL), row),
                   pl.BlockSpec((TILE, D_MODEL), row),
                   pl.BlockSpec((TILE, 128), row)],
        out_shape=[jax.ShapeDtypeStruct((nt, D_MODEL), F32),
                   jax.ShapeDtypeStruct((nt, D_MODEL), BF16),
                   jax.ShapeDtypeStruct((nt, 128), F32)],
        compiler_params=_params(1),
        name="postmix",
    )(x, o_att, y, bonus, g, mod_l, vec(p["g_post_mix"]), vec(p["g_pre_ffn"]),
      vec(p["ln_g"]), vec(p["ln_b"]), p["w_o"], consts["bd"], router_pad)


def _ffn_kernel(h_ref, x1_ref, gates_ref, mod_ref, gpost_ref, w1_ref, w3_ref, w2_ref, o_ref,
                acc_ref):
    e = pl.program_id(1)
    f = pl.program_id(2)
    first = (e == 0) & (f == 0)
    last = (e == pl.num_programs(1) - 1) & (f == pl.num_programs(2) - 1)

    @pl.when(first)
    def _():
        acc_ref[...] = jnp.zeros_like(acc_ref)

    h = h_ref[...]
    a = jnp.dot(h, w1_ref[0], preferred_element_type=F32)
    b = jnp.dot(h, w3_ref[0], preferred_element_type=F32)
    t = (a * _sigmoid(a)) * b
    lane = lax.broadcasted_iota(jnp.int32, gates_ref.shape, 1)
    gate = jnp.sum(jnp.where(lane == e, gates_ref[...], 0.0), -1, keepdims=True)
    acc_ref[...] += gate * jnp.dot(t.astype(BF16), w2_ref[0], preferred_element_type=F32)

    @pl.when(last)
    def _():
        m = mod_ref[0]
        o_ref[...] = x1_ref[...] + m[5:6] * (_rms(acc_ref[...]) * gpost_ref[...])


def _ffn(h, x1, gates, mod_l, g_post, w1, w3, w2, mod_map_ffn):
    nt = h.shape[0]
    n_e, _, d_ff = w1.shape
    n_f = d_ff // FF_BLOCK
    row = lambda i, e, f: (i, 0)
    return pl.pallas_call(
        _ffn_kernel,
        grid=(nt // FFN_TILE, n_e, n_f),
        in_specs=[pl.BlockSpec((FFN_TILE, D_MODEL), row),
                  pl.BlockSpec((FFN_TILE, D_MODEL), row),
                  pl.BlockSpec((FFN_TILE, 128), row),
                  pl.BlockSpec((1, 6, D_MODEL), lambda i, e, f: (mod_map_ffn(i), 0, 0)),
                  pl.BlockSpec((1, D_MODEL), lambda i, e, f: (0, 0)),
                  pl.BlockSpec((1, D_MODEL, FF_BLOCK), lambda i, e, f: (e, 0, f)),
                  pl.BlockSpec((1, D_MODEL, FF_BLOCK), lambda i, e, f: (e, 0, f)),
                  pl.BlockSpec((1, FF_BLOCK, D_MODEL), lambda i, e, f: (e, f, 0))],
        out_specs=pl.BlockSpec((FFN_TILE, D_MODEL), row),
        out_shape=jax.ShapeDtypeStruct((nt, D_MODEL), F32),
        scratch_shapes=[pltpu.VMEM((FFN_TILE, D_MODEL), F32)],
        compiler_params=_params(1, 2),
        name="ffn",
    )(h, x1, gates, mod_l, g_post.reshape(1, D_MODEL), w1, w3, w2)


def _rope_tables(seq_len):
    half = HEAD_DIM // 2
    nf = half // 2
    inv = ROPE_THETA ** (-jnp.arange(nf, dtype=F32) / nf)
    t = jnp.arange(seq_len)
    row = (t // GRID_W).astype(F32)
    col = (t % GRID_W).astype(F32)
    lane = jnp.arange(PAIR)
    pos = jnp.where(((lane % HEAD_DIM) // half == 0)[None, :], row[:, None], col[:, None])
    ang = pos * inv[lane % nf][None, :]
    first = ((lane % half) < nf)[None, :]
    return jnp.cos(ang), jnp.where(first, -jnp.sin(ang), jnp.sin(ang))


def _constants():
    r = jnp.arange(RW)
    bd = (r[:, None] // R_HEAD == r[None, :] // R_HEAD).astype(BF16)
    t = jnp.arange(TILE)
    same = t[:, None] // CHUNK == t[None, :] // CHUNK
    tri = jnp.stack([same & (t[None, :] <= t[:, None]), same & (t[None, :] >= t[:, None])])
    return {"bd": bd, "tri": tri.astype(BF16)}


def _pad_rows(w2, d, width):
    z = jnp.zeros_like(w2[0])
    return jnp.stack([jnp.concatenate([w2[0], z], 0), jnp.concatenate([z, w2[1]], 0)]).astype(BF16)


def kernel(x_prompt, x_sample, cache_k, cache_v, state_rwkv, c, c_ctx, w_mod, b_mod, g_pre_mix, g_post_mix, g_pre_ffn, g_post_ffn, w_in, mu_shift, w_o, attn_sink, rw_w0, rw_w1, rw_w2, rw_a0, rw_a1, rw_a2, rw_g1, rw_g2, rw_k_k, rw_k_a, rw_r_k, rw_ln_g, rw_ln_b, ffn_w1, ffn_w3, ffn_w2, moe_router, moe_w1, moe_w3, moe_w2):
    n_ctx, s_ctx, _ = x_prompt.shape
    n_lat, s_lat, _ = x_sample.shape
    n_past = cache_k.shape[2]
    assert s_ctx == TILE and s_lat % TILE == 0 and (n_ctx * s_ctx) % s_lat == 0
    nt_ctx = n_ctx * s_ctx
    n_ctx_tiles = nt_ctx // TILE
    tiles_per_lat = s_lat // TILE
    mod_map = functools.partial(_mod_row, n_ctx_tiles=n_ctx_tiles, tiles_per_lat=tiles_per_lat)
    ffn_per = FFN_TILE // TILE
    mod_map_ffn = lambda i: mod_map(i * ffn_per)

    x = jnp.concatenate([x_prompt.reshape(nt_ctx, D_MODEL), x_sample.reshape(n_lat * s_lat, D_MODEL)], 0)
    n_cond = 8
    cond = jnp.concatenate([c_ctx[None, :], c, jnp.zeros((n_cond - 1 - n_lat, D_MODEL), F32)], 0)
    mods = _modulation(cond, w_mod, b_mod).reshape(DEPTH, n_cond, 6, D_MODEL)

    consts = _constants()
    cos_t, sin_t = _rope_tables(s_lat)
    zeros_state = jnp.zeros((n_ctx, 2, R_HEADS, R_HEAD, R_HEAD), F32)

    new_k, new_v, new_s = [], [], []
    for l in range(DEPTH):
        w_cat = jnp.concatenate(
            [w_in[l], rw_w1[l, 0], rw_w1[l, 1], rw_a1[l, 0], rw_a1[l, 1], rw_g1[l]], 1).astype(BF16)
        p = {"mu": mu_shift[l].reshape(1, -1), "w0": rw_w0[l], "a0": rw_a0[l],
             "w2": _pad_rows(rw_w2[l], 0, LORA_W), "a2": _pad_rows(rw_a2[l], 0, LORA_A),
             "g2": rw_g2[l].astype(BF16), "k_k": rw_k_k[l].reshape(1, -1),
             "k_a": rw_k_a[l].reshape(1, -1), "r_k": rw_r_k[l].reshape(1, -1),
             "ln_g": rw_ln_g[l], "ln_b": rw_ln_b[l], "w_o": w_o[l].astype(BF16),
             "g_post_mix": g_post_mix[l], "g_pre_ffn": g_pre_ffn[l]}
        sink_rows = jnp.broadcast_to(attn_sink[l][:, None], (N_HEADS, 128))

        q, kv, rkv, lora = _inproj(x, mods[l], g_pre_mix[l], w_cat, mod_map)
        v_b, al, be, ka, rh, wl, bonus, g = _prep(rkv, lora, p, consts, n_ctx_tiles, tiles_per_lat)

        y_ctx, s_ctx_new = _scan(al, be, ka, rh, v_b, wl, zeros_state, 0, n_ctx, s_ctx)
        y_lat, _ = _scan(al, be, ka, rh, v_b, wl, state_rwkv[:, l], nt_ctx, n_lat, s_lat)
        y = jnp.concatenate([y_ctx, y_lat], 1)

        o_ctx = _attn_ctx(q, kv, sink_rows, n_ctx, s_ctx)
        o_lat = _attn_lat(q, kv, cache_k[:, l].reshape(n_lat, n_past, ATT_KV),
                          cache_v[:, l].reshape(n_lat, n_past, ATT_KV), cos_t, sin_t, sink_rows,
                          nt_ctx, n_lat, s_lat)
        o_att = jnp.concatenate([o_ctx, o_lat], 0)

        if l % 2 == 0:
            i = l // 2
            router_pad = None
            w1 = ffn_w1[i].astype(BF16)[None]
            w3 = ffn_w3[i].astype(BF16)[None]
            w2 = ffn_w2[i].astype(BF16)[None]
        else:
            i = l // 2
            router_pad = jnp.pad(moe_router[i], ((0, 0), (0, 128 - N_EXPERTS)))
            w1 = moe_w1[i].astype(BF16)
            w3 = moe_w3[i].astype(BF16)
            w2 = moe_w2[i].astype(BF16)
        x1, h, gates = _postmix(x, o_att, y, bonus, g, mods[l], p, consts, mod_map, router_pad)
        x = _ffn(h, x1, gates, mods[l], g_post_ffn[l], w1, w3, w2, mod_map_ffn)

        new_k.append(kv[:nt_ctx, :ATT_KV].reshape(n_ctx, s_ctx, KV_HEADS, HEAD_DIM))
        new_v.append(kv[:nt_ctx, ATT_KV:].reshape(n_ctx, s_ctx, KV_HEADS, HEAD_DIM))
        new_s.append(s_ctx_new)

    y_prompt = x[:nt_ctx].reshape(n_ctx, s_ctx, D_MODEL)
    y_sample = x[nt_ctx:].reshape(n_lat, s_lat, D_MODEL)
    return (y_prompt, y_sample, jnp.stack(new_k, 1), jnp.stack(new_v, 1), jnp.stack(new_s, 1))
```

```python
import functools

import jax
import jax.numpy as jnp
from jax import lax
from jax.experimental import pallas as pl
from jax.experimental.pallas import tpu as pltpu

F32 = jnp.float32
BF16 = jnp.bfloat16

D_MODEL = 1024
DEPTH = 4
GRID_W = 64
N_HEADS = 8
KV_HEADS = 2
HEAD_DIM = 64
WINDOW = 128
ROPE_THETA = 10000.0
R_HEADS = 8
R_HEAD = 64
RW = R_HEADS * R_HEAD
LORA_W = 64
LORA_A = 64
LORA_G = 128
LORA_ALL = 2 * LORA_W + 2 * LORA_A + LORA_G
ATT_Q = N_HEADS * HEAD_DIM
ATT_KV = KV_HEADS * HEAD_DIM
D_IN = ATT_Q + 2 * ATT_KV + 3 * RW
D_FF = 2816
N_EXPERTS = 8
D_FF_EXPERT = 1408
NORM_EPS = 1e-6
GN_EPS = 64e-5
NEG_INF = -1e30

TILE = 256
CHUNK = 64
PAIR = 2 * R_HEAD
N_PAIRS = R_HEADS // 2
FFN_TILE = 512
FF_BLOCK = 1408
MOD_COLS = 1536
VMEM_LIMIT = 48 * 1024 * 1024

_NT = (((1,), (1,)), ((), ()))


def _params(n_parallel, n_arbitrary=0):
    sem = ("parallel",) * n_parallel + ("arbitrary",) * n_arbitrary
    return pltpu.CompilerParams(dimension_semantics=sem, vmem_limit_bytes=VMEM_LIMIT)


def _sigmoid(x):
    return 1.0 / (1.0 + jnp.exp(-x))


def _mm(a, b):
    return jnp.dot(a.astype(BF16), b.astype(BF16), preferred_element_type=F32)


def _mm_nt(a, b):
    return lax.dot_general(a.astype(BF16), b.astype(BF16), _NT, preferred_element_type=F32)


def _split3(x):
    x0 = x.astype(BF16)
    r1 = x - x0.astype(F32)
    x1 = r1.astype(BF16)
    x2 = (r1 - x1.astype(F32)).astype(BF16)
    return x0, x1, x2


def _mm_ones_right(x, ones_bf16):
    x0, x1, x2 = _split3(x)
    d = functools.partial(jnp.dot, preferred_element_type=F32)
    return d(x0, ones_bf16) + d(x1, ones_bf16) + d(x2, ones_bf16)


def _mm_ones_left(ones_bf16, x):
    x0, x1, x2 = _split3(x)
    d = functools.partial(jnp.dot, preferred_element_type=F32)
    return d(ones_bf16, x0) + d(ones_bf16, x1) + d(ones_bf16, x2)


def _rms(x):
    return x * lax.rsqrt(jnp.mean(x * x, -1, keepdims=True) + NORM_EPS)


def _mod_row(i, n_ctx_tiles, tiles_per_lat):
    return jnp.where(i < n_ctx_tiles, 0, 1 + (i - n_ctx_tiles) // tiles_per_lat)


def _mod_kernel(cond_ref, w_ref, b_ref, o_ref):
    c = cond_ref[...]
    s = c * _sigmoid(c)
    o_ref[0] = _mm(s, w_ref[0]) + b_ref[0]


def _modulation(cond, w_mod, b_mod):
    n_cond = cond.shape[0]
    n_col = 6 * D_MODEL // MOD_COLS
    return pl.pallas_call(
        _mod_kernel,
        grid=(DEPTH, n_col),
        in_specs=[pl.BlockSpec((n_cond, D_MODEL), lambda l, j: (0, 0)),
                  pl.BlockSpec((1, D_MODEL, MOD_COLS), lambda l, j: (l, 0, j)),
                  pl.BlockSpec((1, 1, MOD_COLS), lambda l, j: (l, 0, j))],
        out_specs=pl.BlockSpec((1, n_cond, MOD_COLS), lambda l, j: (l, 0, j)),
        out_shape=jax.ShapeDtypeStruct((DEPTH, n_cond, 6 * D_MODEL), F32),
        compiler_params=_params(2),
        name="modulation",
    )(cond, w_mod, b_mod.reshape(DEPTH, 1, 6 * D_MODEL))


def _inproj_kernel(x_ref, mod_ref, g_ref, w_ref, q_ref, kv_ref, rkv_ref, lora_ref):
    m = mod_ref[0]
    h = _rms(x_ref[...]) * g_ref[...]
    h = h * (1.0 + m[1:2]) + m[0:1]
    u = jnp.dot(h.astype(BF16), w_ref[...], preferred_element_type=F32)
    q_ref[...] = u[:, :ATT_Q]
    kv_ref[...] = u[:, ATT_Q:ATT_Q + 2 * ATT_KV]
    rkv_ref[...] = u[:, ATT_Q + 2 * ATT_KV:D_IN]
    lora_ref[...] = u[:, D_IN:]


def _inproj(x, mod_l, g, w_cat, layer, mod_map):
    nt = x.shape[0]
    row = lambda i: (i, 0)
    const = lambda i: (0, 0)
    return pl.pallas_call(
        _inproj_kernel,
        grid=(nt // TILE,),
        in_specs=[pl.BlockSpec((TILE, D_MODEL), row),
                  pl.BlockSpec((1, 6, D_MODEL), lambda i: (mod_map(i), 0, 0)),
                  pl.BlockSpec((1, D_MODEL), const),
                  pl.BlockSpec((None, D_MODEL, D_IN + LORA_ALL), lambda i: (layer, 0, 0))],
        out_specs=[pl.BlockSpec((TILE, ATT_Q), row),
                   pl.BlockSpec((TILE, 2 * ATT_KV), row),
                   pl.BlockSpec((TILE, 3 * RW), row),
                   pl.BlockSpec((TILE, LORA_ALL), row)],
        out_shape=[jax.ShapeDtypeStruct((nt, ATT_Q), F32),
                   jax.ShapeDtypeStruct((nt, 2 * ATT_KV), F32),
                   jax.ShapeDtypeStruct((nt, 3 * RW), F32),
                   jax.ShapeDtypeStruct((nt, LORA_ALL), F32)],
        compiler_params=_params(1),
        name="inproj",
    )(x, mod_l, g.reshape(1, D_MODEL), w_cat)


def _prep_kernel(rkv_ref, prev_ref, next_ref, lora_ref, mu_ref, w0_ref, w2_ref, a0_ref, a2_ref,
                 g2_ref, kk_ref, ka_ref, rk_ref, bd_ref, tri_ref,
                 v_out, al_out, be_out, ka_out, rh_out, wl_out, bonus_out, g_out,
                 *, n_ctx_tiles, tiles_per_lat):
    i = pl.program_id(0)
    j = (i - n_ctx_tiles) % tiles_per_lat
    is_ctx = i < n_ctx_tiles
    first = is_ctx | (j == 0)
    last = is_ctx | (j == tiles_per_lat - 1)

    u = rkv_ref[...]
    row = lax.broadcasted_iota(jnp.int32, (TILE, 1), 0)
    p_row = prev_ref[7:8, :] * (1.0 - first.astype(F32))
    n_row = next_ref[0:1, :] * (1.0 - last.astype(F32))
    prev = jnp.where(row == 0, p_row, pltpu.roll(u, 1, 0))
    nxt = jnp.where(row == TILE - 1, n_row, pltpu.roll(u, TILE - 1, 0))
    us = u + mu_ref[...] * (0.5 * (prev + nxt) - u)
    r = us[:, :RW]
    k = us[:, RW:2 * RW]
    v = us[:, 2 * RW:]

    bd = bd_ref[...]
    kk = k * kk_ref[...]
    kk = kk * lax.rsqrt(_mm_ones_right(kk * kk, bd) + 1e-12)

    lora = lora_ref[...]
    lw_in = jnp.tanh(lora[:, :2 * LORA_W])
    la_in = lora[:, 2 * LORA_W:2 * LORA_W + 2 * LORA_A]
    bonus = jnp.zeros_like(v)
    for d in range(2):
        w_pre = w0_ref[d:d + 1, :] + _mm(lw_in, w2_ref[d])
        lw = -jnp.exp(-0.5) * _sigmoid(w_pre)
        a = _sigmoid(a0_ref[d:d + 1, :] + _mm(la_in, a2_ref[d]))
        kd = k * (1.0 + (a - 1.0) * ka_ref[...])
        bonus = bonus + _mm_ones_right(r * kd * rk_ref[...], bd) * v
        cw = _mm_ones_left(tri_ref[d], lw)
        e_neg = jnp.exp(-cw)
        al_out[d] = (kk * jnp.exp(cw - lw)).astype(BF16)
        be_out[d] = (kk * a * e_neg).astype(BF16)
        ka_out[d] = (kd * e_neg).astype(BF16)
        rh_out[d] = (r * jnp.exp(cw)).astype(BF16)
        for c in range(TILE // CHUNK):
            end = c * CHUNK + (CHUNK - 1 if d == 0 else 0)
            wl_out[0, 4 * d + c:4 * d + c + 1, :] = jnp.exp(cw[end:end + 1, :])
    v_out[...] = v.astype(BF16)
    bonus_out[...] = bonus
    g_out[...] = _mm(_sigmoid(lora[:, 2 * LORA_W + 2 * LORA_A:]), g2_ref[...])


def _prep(rkv, lora, p, consts, n_ctx_tiles, tiles_per_lat):
    nt = rkv.shape[0]
    n_tiles = nt // TILE
    row = lambda i: (i, 0)
    drow = lambda i: (0, i, 0)
    c2 = lambda i: (0, 0)
    c3 = lambda i: (0, 0, 0)
    hb = TILE // 8
    kern = functools.partial(_prep_kernel, n_ctx_tiles=n_ctx_tiles, tiles_per_lat=tiles_per_lat)
    dshape = jax.ShapeDtypeStruct((2, nt, RW), BF16)
    return pl.pallas_call(
        kern,
        grid=(n_tiles,),
        in_specs=[pl.BlockSpec((TILE, 3 * RW), row),
                  pl.BlockSpec((8, 3 * RW), lambda i: (jnp.maximum(i * hb - 1, 0), 0)),
                  pl.BlockSpec((8, 3 * RW), lambda i: (jnp.minimum((i + 1) * hb, nt // 8 - 1), 0)),
                  pl.BlockSpec((TILE, LORA_ALL), row),
                  pl.BlockSpec((1, 3 * RW), c2),
                  pl.BlockSpec((2, RW), c2),
                  pl.BlockSpec((2, 2 * LORA_W, RW), c3),
                  pl.BlockSpec((2, RW), c2),
                  pl.BlockSpec((2, 2 * LORA_A, RW), c3),
                  pl.BlockSpec((LORA_G, RW), c2),
                  pl.BlockSpec((1, RW), c2),
                  pl.BlockSpec((1, RW), c2),
                  pl.BlockSpec((1, RW), c2),
                  pl.BlockSpec((RW, RW), c2),
                  pl.BlockSpec((2, TILE, TILE), c3)],
        out_specs=[pl.BlockSpec((TILE, RW), row),
                   pl.BlockSpec((2, TILE, RW), drow),
                   pl.BlockSpec((2, TILE, RW), drow),
                   pl.BlockSpec((2, TILE, RW), drow),
                   pl.BlockSpec((2, TILE, RW), drow),
                   pl.BlockSpec((1, 8, RW), lambda i: (i, 0, 0)),
                   pl.BlockSpec((TILE, RW), row),
                   pl.BlockSpec((TILE, RW), row)],
        out_shape=[jax.ShapeDtypeStruct((nt, RW), BF16), dshape, dshape, dshape, dshape,
                   jax.ShapeDtypeStruct((n_tiles, 8, RW), F32),
                   jax.ShapeDtypeStruct((nt, RW), F32),
                   jax.ShapeDtypeStruct((nt, RW), F32)],
        compiler_params=_params(1),
        name="rwkv_prep",
    )(rkv, rkv, rkv, lora, p["mu"], p["w0"], p["w2"], p["a0"], p["a2"], p["g2"],
      p["k_k"], p["k_a"], p["r_k"], consts["bd"], consts["tri"])


def _scan_kernel(al0, be0, ka0, rh0, v0, wl0, al1, be1, ka1, rh1, v1, wl1, s0_ref,
                 y0_ref, y1_ref, st_ref, s_scr, *, n_ctx_tiles, tiles_per_lat):
    i = pl.program_id(0)
    is_ctx = i < n_ctx_tiles
    seq_start = (i - n_ctx_tiles) % tiles_per_lat == 0

    @pl.when(is_ctx)
    def _():
        s_scr[...] = jnp.zeros_like(s_scr)

    @pl.when(jnp.logical_not(is_ctx) & seq_start)
    def _():
        s_scr[...] = s0_ref[...]

    lo = lax.broadcasted_iota(jnp.int32, (CHUNK, PAIR), 1) < R_HEAD
    ri = lax.broadcasted_iota(jnp.int32, (PAIR, PAIR), 0)
    ci = lax.broadcasted_iota(jnp.int32, (PAIR, PAIR), 1)
    same = (ri // CHUNK) == (ci // CHUNK)
    rt = ri % CHUNK
    ct = ci % CHUNK
    before = [(ct < rt) & same, (ct > rt) & same]
    upto = [(ct <= rt) & same, (ct >= rt) & same]
    n_sq = CHUNK.bit_length() - 2
    n_chunks = TILE // CHUNK
    refs = [(al0, be0, ka0, rh0, v0, wl0, y0_ref), (al1, be1, ka1, rh1, v1, wl1, y1_ref)]

    def expand(x):
        zero = jnp.zeros_like(x)
        return jnp.concatenate([jnp.where(lo, x, zero), jnp.where(lo, zero, x)], 0)

    def chunk(c, carry):
        chains = []
        for d in range(2):
            al, be, ka, rh, v, wl, y_ref = refs[d]
            ce = c if d == 0 else n_chunks - 1 - c
            rows = pl.ds(pl.multiple_of(ce * CHUNK, CHUNK), CHUNK)
            wl_all = wl[pl.ds(n_chunks * d + ce, 1), :]
            for p in range(N_PAIRS):
                lanes = slice(PAIR * p, PAIR * (p + 1))
                chains.append(dict(
                    d=d, p=p, rows=rows, lanes=lanes, y_ref=y_ref, wl=wl_all[:, lanes],
                    a=expand(al[rows, lanes]), b=expand(be[rows, lanes]),
                    k=expand(ka[rows, lanes]), r=expand(rh[rows, lanes]),
                    v=expand(v[rows, lanes]), s=s_scr[d, p]))

        zero = jnp.zeros((PAIR, PAIR), F32)
        for ch in chains:
            ar = jnp.concatenate([ch["a"], ch["r"]], 0)
            bks = jnp.concatenate([ch["b"], ch["k"], ch["s"].astype(BF16)], 0)
            ch["gram"] = lax.dot_general(ar, bks, _NT, preferred_element_type=F32)
        for ch in chains:
            g, d = ch["gram"], ch["d"]
            ch["m_ab"] = jnp.where(before[d], g[:PAIR, :PAIR], zero)
            ch["m_rb"] = jnp.where(upto[d], g[PAIR:, :PAIR], zero)
            m_ak = jnp.where(before[d], g[:PAIR, PAIR:2 * PAIR], zero)
            m_rk = jnp.where(upto[d], g[PAIR:, PAIR:2 * PAIR], zero)
            ch["mv"] = _mm(jnp.concatenate([m_ak, m_rk], 0), ch["v"])

        for ch in chains:
            ch["x"] = ch["gram"][:PAIR, 2 * PAIR:] + ch["mv"][:PAIR]
            ch["pw"] = ch["m_ab"]
        for lvl in range(n_sq + 1):
            for ch in chains:
                if lvl < n_sq:
                    px = _mm(ch["pw"], jnp.concatenate([ch["pw"], ch["x"]], 1))
                    ch["pw"] = px[:, :PAIR]
                    step = px[:, PAIR:]
                else:
                    step = _mm(ch["pw"], ch["x"])
                ch["x"] = ch["x"] - step if lvl == 0 else ch["x"] + step

        for ch in chains:
            z = -ch["x"]
            y2 = ch["gram"][PAIR:, 2 * PAIR:] + ch["mv"][PAIR:] + _mm(ch["m_rb"], z)
            ch["y"] = y2[:CHUNK] + y2[CHUNK:]
            ch["zv_t"] = jnp.concatenate([z, ch["v"].astype(F32)], 0).T
        for ch in chains:
            bk = jnp.concatenate([ch["b"], ch["k"]], 0)
            ch["s_new"] = (ch["s"] + _mm(ch["zv_t"], bk)) * ch["wl"]
        for ch in chains:
            ch["y_ref"][ch["rows"], ch["lanes"]] = ch["y"]
            s_scr[ch["d"], ch["p"]] = ch["s_new"]
        return carry

    lax.fori_loop(0, n_chunks, chunk, 0)
    st_ref[...] = s_scr[...]


def _expand_state(s0):
    n = s0.shape[0]
    s = s0.reshape(n, 2, N_PAIRS, 2, R_HEAD, R_HEAD)
    z = jnp.zeros_like(s[:, :, :, 0])
    top = jnp.concatenate([s[:, :, :, 0], z], -1)
    bot = jnp.concatenate([z, s[:, :, :, 1]], -1)
    return jnp.concatenate([top, bot], -2)


def _collapse_state(st):
    n = st.shape[0]
    h0 = st[:, :, :, :R_HEAD, :R_HEAD]
    h1 = st[:, :, :, R_HEAD:, R_HEAD:]
    return jnp.stack([h0, h1], 3).reshape(n, 2, R_HEADS, R_HEAD, R_HEAD)


def _scan(al, be, ka, rh, v, wl, s0_lat, n_ctx_tiles, tiles_per_lat):
    nt = v.shape[0]
    n_tiles = nt // TILE
    n_lat = s0_lat.shape[0]

    def mirror(i):
        j = (i - n_ctx_tiles) % tiles_per_lat
        return jnp.where(i < n_ctx_tiles, i, i - j + (tiles_per_lat - 1 - j))

    lat_seq = lambda i: jnp.clip((i - n_ctx_tiles) // tiles_per_lat, 0, n_lat - 1)
    fwd3 = pl.BlockSpec((None, TILE, RW), lambda i: (0, i, 0))
    bwd3 = pl.BlockSpec((None, TILE, RW), lambda i: (1, mirror(i), 0))
    fwd2 = pl.BlockSpec((TILE, RW), lambda i: (i, 0))
    bwd2 = pl.BlockSpec((TILE, RW), lambda i: (mirror(i), 0))
    wl_f = pl.BlockSpec((None, 8, RW), lambda i: (i, 0, 0))
    wl_b = pl.BlockSpec((None, 8, RW), lambda i: (mirror(i), 0, 0))
    state_blk = (None, 2, N_PAIRS, PAIR, PAIR)
    kern = functools.partial(_scan_kernel, n_ctx_tiles=n_ctx_tiles, tiles_per_lat=tiles_per_lat)
    y0, y1, st = pl.pallas_call(
        kern,
        grid=(n_tiles,),
        in_specs=[fwd3, fwd3, fwd3, fwd3, fwd2, wl_f, bwd3, bwd3, bwd3, bwd3, bwd2, wl_b,
                  pl.BlockSpec(state_blk, lambda i: (lat_seq(i), 0, 0, 0, 0))],
        out_specs=[fwd2, bwd2,
                   pl.BlockSpec(state_blk, lambda i: (jnp.minimum(i, n_ctx_tiles), 0, 0, 0, 0))],
        out_shape=[jax.ShapeDtypeStruct((nt, RW), F32), jax.ShapeDtypeStruct((nt, RW), F32),
                   jax.ShapeDtypeStruct((n_ctx_tiles + 1, 2, N_PAIRS, PAIR, PAIR), F32)],
        scratch_shapes=[pltpu.VMEM((2, N_PAIRS, PAIR, PAIR), F32)],
        compiler_params=_params(0, 1),
        name="wkv_scan",
    )(al, be, ka, rh, v, wl, al, be, ka, rh, v, wl, _expand_state(s0_lat))
    return y0, y1, _collapse_state(st[:n_ctx_tiles])


def _dup_halves(x, lo):
    xr = pltpu.roll(x, HEAD_DIM, 1)
    return jnp.where(lo, x, xr), jnp.where(lo, xr, x)


def _stack_heads(qp, lo):
    zero = jnp.zeros_like(qp)
    return jnp.concatenate([jnp.where(lo, qp, zero), jnp.where(lo, zero, qp)], 0)


def _sink_col(sink_ref, p, rows):
    r = lax.broadcasted_iota(jnp.int32, (2 * rows, 1), 0)
    return jnp.where(r < rows, sink_ref[2 * p:2 * p + 1, 0:1], sink_ref[2 * p + 1:2 * p + 2, 0:1])


def _attn_ctx_kernel(q_ref, kv_ref, sink_ref, o_ref):
    rows = q_ref.shape[0]
    lo = lax.broadcasted_iota(jnp.int32, (rows, PAIR), 1) < HEAD_DIM
    k_dup = _dup_halves(kv_ref[:, :ATT_KV], lo)
    v_dup = _dup_halves(kv_ref[:, ATT_KV:], lo)
    scale = HEAD_DIM ** -0.5
    for p in range(N_HEADS // 2):
        g = (2 * p) // (N_HEADS // KV_HEADS)
        qs = _stack_heads(q_ref[:, PAIR * p:PAIR * (p + 1)] * scale, lo)
        s = _mm_nt(qs, k_dup[g])
        sk = _sink_col(sink_ref, p, rows)
        m = jnp.maximum(jnp.max(s, -1, keepdims=True), sk)
        e = jnp.exp(s - m)
        den = jnp.sum(e, -1, keepdims=True) + jnp.exp(sk - m)
        o2 = _mm(e / den, v_dup[g])
        o_ref[:, PAIR * p:PAIR * (p + 1)] = jnp.where(lo, o2[:rows], o2[rows:])


def _attn_ctx(q, kv, sink_rows, n_seq, seq_len):
    return pl.pallas_call(
        _attn_ctx_kernel,
        grid=(n_seq,),
        in_specs=[pl.BlockSpec((seq_len, ATT_Q), lambda b: (b, 0)),
                  pl.BlockSpec((seq_len, 2 * ATT_KV), lambda b: (b, 0)),
                  pl.BlockSpec((N_HEADS, 128), lambda b: (0, 0))],
        out_specs=pl.BlockSpec((seq_len, ATT_Q), lambda b: (b, 0)),
        out_shape=jax.ShapeDtypeStruct((n_seq * seq_len, ATT_Q), F32),
        compiler_params=_params(1),
        name="attn_ctx",
    )(q, kv, sink_rows)


def _rope(x, cos, sgn_sin, first16):
    sw = jnp.where(first16, pltpu.roll(x, PAIR - 16, 1), pltpu.roll(x, 16, 1))
    return x * cos + sw * sgn_sin


def _attn_lat_kernel(q_ref, kv_ref, ck_ref, cv_ref, cos_ref, sin_ref, sink_ref, o_ref,
                     kd_scr, vd_scr, ckd_scr, cvd_scr, *, seq_len):
    i = pl.program_id(1)
    qb = q_ref.shape[0]
    n_blk = seq_len // qb

    @pl.when(i == 0)
    def _():
        lo_s = lax.broadcasted_iota(jnp.int32, (seq_len, PAIR), 1) < HEAD_DIM
        f16_s = (lax.broadcasted_iota(jnp.int32, (seq_len, PAIR), 1) % 32) < 16
        k_r = _rope(kv_ref[:, :ATT_KV], cos_ref[...], sin_ref[...], f16_s)
        k0, k1 = _dup_halves(k_r, lo_s)
        v0, v1 = _dup_halves(kv_ref[:, ATT_KV:], lo_s)
        kd_scr[0] = k0.astype(BF16)
        kd_scr[1] = k1.astype(BF16)
        vd_scr[0] = v0.astype(BF16)
        vd_scr[1] = v1.astype(BF16)
        lo_c = lax.broadcasted_iota(jnp.int32, ck_ref.shape[1:], 1) < HEAD_DIM
        c0, c1 = _dup_halves(ck_ref[0], lo_c)
        ckd_scr[0] = c0.astype(BF16)
        ckd_scr[1] = c1.astype(BF16)
        c0, c1 = _dup_halves(cv_ref[0], lo_c)
        cvd_scr[0] = c0.astype(BF16)
        cvd_scr[1] = c1.astype(BF16)

    lo = lax.broadcasted_iota(jnp.int32, (qb, PAIR), 1) < HEAD_DIM
    f16 = (lax.broadcasted_iota(jnp.int32, (qb, PAIR), 1) % 32) < 16
    q0 = pl.multiple_of(i * qb, qb)
    cos = cos_ref[pl.ds(q0, qb), :]
    sin = sin_ref[pl.ds(q0, qb), :]
    qpos = i * qb + lax.broadcasted_iota(jnp.int32, (2 * qb, qb), 0) % qb
    kcol = lax.broadcasted_iota(jnp.int32, (2 * qb, qb), 1)
    scale = HEAD_DIM ** -0.5
    n_past = ck_ref.shape[1]

    for p in range(N_HEADS // 2):
        g = (2 * p) // (N_HEADS // KV_HEADS)
        qp = _rope(q_ref[:, PAIR * p:PAIR * (p + 1)], cos, sin, f16) * scale
        qs = _stack_heads(qp, lo).astype(BF16)
        parts = [_mm_nt(qs, ckd_scr[g])]
        vals = [cvd_scr[g]]
        for jj in range(3):
            j = i - 1 + jj
            jc = jnp.clip(j, 0, n_blk - 1)
            k0 = pl.multiple_of(jc * qb, qb)
            kpos = jnp.where(j == jc, jc * qb, -4 * seq_len) + kcol
            ok = jnp.abs(qpos - kpos) <= WINDOW
            sw = _mm_nt(qs, kd_scr[g, pl.ds(k0, qb), :])
            parts.append(jnp.where(ok, sw, NEG_INF))
            vals.append(vd_scr[g, pl.ds(k0, qb), :])
        s = jnp.concatenate(parts, 1)
        sk = _sink_col(sink_ref, p, qb)
        m = jnp.maximum(jnp.max(s, -1, keepdims=True), sk)
        e = jnp.exp(s - m)
        den = jnp.sum(e, -1, keepdims=True) + jnp.exp(sk - m)
        pr = (e / den).astype(BF16)
        o2 = jnp.dot(pr[:, :n_past], vals[0], preferred_element_type=F32)
        for jj in range(3):
            c0 = n_past + jj * qb
            o2 = o2 + jnp.dot(pr[:, c0:c0 + qb], vals[1 + jj], preferred_element_type=F32)
        o_ref[:, PAIR * p:PAIR * (p + 1)] = jnp.where(lo, o2[:qb], o2[qb:])


def _attn_lat(q, kv, ck, cv, layer, cos_t, sin_t, sink_rows, row0, n_seq, seq_len):
    qb = WINDOW
    n_blk = seq_len // qb
    blk0 = row0 // qb
    seq0 = row0 // seq_len
    n_past = ck.shape[2]
    kern = functools.partial(_attn_lat_kernel, seq_len=seq_len)
    return pl.pallas_call(
        kern,
        grid=(n_seq, n_blk),
        in_specs=[pl.BlockSpec((qb, ATT_Q), lambda b, i: (blk0 + b * n_blk + i, 0)),
                  pl.BlockSpec((seq_len, 2 * ATT_KV), lambda b, i: (seq0 + b, 0)),
                  pl.BlockSpec((1, None, n_past, ATT_KV), lambda b, i: (b, layer, 0, 0)),
                  pl.BlockSpec((1, None, n_past, ATT_KV), lambda b, i: (b, layer, 0, 0)),
                  pl.BlockSpec((seq_len, PAIR), lambda b, i: (0, 0)),
                  pl.BlockSpec((seq_len, PAIR), lambda b, i: (0, 0)),
                  pl.BlockSpec((N_HEADS, 128), lambda b, i: (0, 0))],
        out_specs=pl.BlockSpec((qb, ATT_Q), lambda b, i: (b * n_blk + i, 0)),
        out_shape=jax.ShapeDtypeStruct((n_seq * seq_len, ATT_Q), F32),
        scratch_shapes=[pltpu.VMEM((2, seq_len, PAIR), BF16), pltpu.VMEM((2, seq_len, PAIR), BF16),
                        pltpu.VMEM((2, n_past, PAIR), BF16), pltpu.VMEM((2, n_past, PAIR), BF16)],
        compiler_params=_params(1, 1),
        name="attn_lat",
    )(q, kv, ck, cv, cos_t, sin_t, sink_rows)


def _postmix_kernel(x_ref, octx_ref, olat_ref, y0_ref, y1_ref, bonus_ref, g_ref, mod_ref,
                    gpost_ref, gpre_ref, lng_ref, lnb_ref, wo_ref, bd_ref, router_ref,
                    x1_ref, h_ref, gates_ref, *, with_router, n_ctx_tiles):
    m = mod_ref[0]
    bd = bd_ref[...]
    tile = lax.broadcasted_iota(jnp.int32, octx_ref.shape, 0) * 0 + pl.program_id(0)
    o_att = jnp.where(tile < n_ctx_tiles, octx_ref[...], olat_ref[...])
    y = y0_ref[...] + y1_ref[...]
    inv_n = 1.0 / R_HEAD
    mu = _mm_ones_right(y, bd) * inv_n
    yc = y - mu
    var = _mm_ones_right(yc * yc, bd) * inv_n
    yn = yc * lax.rsqrt(var + GN_EPS)
    o_rw = (yn * lng_ref[...] + lnb_ref[...] + bonus_ref[...]) * g_ref[...]
    o = (jnp.dot(o_att.astype(BF16), wo_ref[:ATT_Q, :], preferred_element_type=F32)
         + jnp.dot(o_rw.astype(BF16), wo_ref[ATT_Q:, :], preferred_element_type=F32))
    x1 = x_ref[...] + m[2:3] * (_rms(o) * gpost_ref[...])
    x1_ref[...] = x1
    h = (_rms(x1) * gpre_ref[...]) * (1.0 + m[4:5]) + m[3:4]
    h_ref[...] = h.astype(BF16)
    if with_router:
        logits = jnp.dot(h, router_ref[...], preferred_element_type=F32,
                         precision=lax.Precision.HIGHEST)
        lane = lax.broadcasted_iota(jnp.int32, logits.shape, 1)
        valid = lane < N_EXPERTS
        logits = jnp.where(valid, logits, NEG_INF)
        e = jnp.exp(logits - jnp.max(logits, -1, keepdims=True))
        probs = e / jnp.sum(e, -1, keepdims=True)
        lane_f = lane.astype(F32)
        p1 = jnp.max(probs, -1, keepdims=True)
        i1 = jnp.min(jnp.where(probs == p1, lane_f, 1e9), -1, keepdims=True)
        rest = jnp.where(lane_f == i1, -1.0, probs)
        p2 = jnp.max(rest, -1, keepdims=True)
        i2 = jnp.min(jnp.where(rest == p2, lane_f, 1e9), -1, keepdims=True)
        sel = (lane_f == i1) | (lane_f == i2)
        gates_ref[...] = jnp.where(sel, probs / (p1 + p2), 0.0)
    else:
        gates_ref[...] = jnp.ones(gates_ref.shape, F32)


def _postmix(x, o_ctx, o_lat, y0, y1, bonus, g, mod_l, p, w_o, layer, consts, mod_map,
             router_pad, n_ctx_tiles):
    nt = x.shape[0]
    row = lambda i: (i, 0)
    c2 = lambda i: (0, 0)
    with_router = router_pad is not None
    if router_pad is None:
        router_pad = jnp.zeros((D_MODEL, 128), F32)
    kern = functools.partial(_postmix_kernel, with_router=with_router, n_ctx_tiles=n_ctx_tiles)
    n_lat_tiles = nt // TILE - n_ctx_tiles
    vec = lambda a: a.reshape(1, -1)
    return pl.pallas_call(
        kern,
        grid=(nt // TILE,),
        in_specs=[pl.BlockSpec((TILE, D_MODEL), row),
                  pl.BlockSpec((TILE, ATT_Q), lambda i: (jnp.minimum(i, n_ctx_tiles - 1), 0)),
                  pl.BlockSpec((TILE, ATT_Q),
                               lambda i: (jnp.clip(i - n_ctx_tiles, 0, n_lat_tiles - 1), 0)),
                  pl.BlockSpec((TILE, RW), row),
                  pl.BlockSpec((TILE, RW), row),
                  pl.BlockSpec((TILE, RW), row),
                  pl.BlockSpec((TILE, RW), row),
                  pl.BlockSpec((1, 6, D_MODEL), lambda i: (mod_map(i), 0, 0)),
                  pl.BlockSpec((1, D_MODEL), c2),
                  pl.BlockSpec((1, D_MODEL), c2),
                  pl.BlockSpec((1, RW), c2),
                  pl.BlockSpec((1, RW), c2),
                  pl.BlockSpec((None, ATT_Q + RW, D_MODEL), lambda i: (layer, 0, 0)),
                  pl.BlockSpec((RW, RW), c2),
                  pl.BlockSpec((D_MODEL, 128), c2)],
        out_specs=[pl.BlockSpec((TILE, D_MODEL), row),
                   pl.BlockSpec((TILE, D_MODEL), row),
                   pl.BlockSpec((TILE, 128), row)],
        out_shape=[jax.ShapeDtypeStruct((nt, D_MODEL), F32),
                   jax.ShapeDtypeStruct((nt, D_MODEL), BF16),
                   jax.ShapeDtypeStruct((nt, 128), F32)],
        compiler_params=_params(1),
        name="postmix",
    )(x, o_ctx, o_lat, y0, y1, bonus, g, mod_l, vec(p["g_post_mix"]), vec(p["g_pre_ffn"]),
      vec(p["ln_g"]), vec(p["ln_b"]), w_o, consts["bd"], router_pad)


def _ffn_kernel(h_ref, x1_ref, gates_ref, mod_ref, gpost_ref, w1_ref, w3_ref, w2_ref, o_ref,
                acc_ref):
    e = pl.program_id(1)
    f = pl.program_id(2)
    first = (e == 0) & (f == 0)
    last = (e == pl.num_programs(1) - 1) & (f == pl.num_programs(2) - 1)

    @pl.when(first)
    def _():
        acc_ref[...] = jnp.zeros_like(acc_ref)

    h = h_ref[...]
    a = jnp.dot(h, w1_ref[0], preferred_element_type=F32)
    b = jnp.dot(h, w3_ref[0], preferred_element_type=F32)
    t = (a * _sigmoid(a)) * b
    lane = lax.broadcasted_iota(jnp.int32, gates_ref.shape, 1)
    gate = jnp.sum(jnp.where(lane == e, gates_ref[...], 0.0), -1, keepdims=True)
    acc_ref[...] += gate * jnp.dot(t.astype(BF16), w2_ref[0], preferred_element_type=F32)

    @pl.when(last)
    def _():
        m = mod_ref[0]
        o_ref[...] = x1_ref[...] + m[5:6] * (_rms(acc_ref[...]) * gpost_ref[...])


def _ffn(h, x1, gates, mod_l, g_post, w1, w3, w2, layer, mod_map_ffn):
    nt = h.shape[0]
    _, n_e, _, d_ff = w1.shape
    n_f = d_ff // FF_BLOCK
    row = lambda i, e, f: (i, 0)
    return pl.pallas_call(
        _ffn_kernel,
        grid=(nt // FFN_TILE, n_e, n_f),
        in_specs=[pl.BlockSpec((FFN_TILE, D_MODEL), row),
                  pl.BlockSpec((FFN_TILE, D_MODEL), row),
                  pl.BlockSpec((FFN_TILE, 128), row),
                  pl.BlockSpec((1, 6, D_MODEL), lambda i, e, f: (mod_map_ffn(i), 0, 0)),
                  pl.BlockSpec((1, D_MODEL), lambda i, e, f: (0, 0)),
                  pl.BlockSpec((None, 1, D_MODEL, FF_BLOCK), lambda i, e, f: (layer, e, 0, f)),
                  pl.BlockSpec((None, 1, D_MODEL, FF_BLOCK), lambda i, e, f: (layer, e, 0, f)),
                  pl.BlockSpec((None, 1, FF_BLOCK, D_MODEL), lambda i, e, f: (layer, e, f, 0))],
        out_specs=pl.BlockSpec((FFN_TILE, D_MODEL), row),
        out_shape=jax.ShapeDtypeStruct((nt, D_MODEL), F32),
        scratch_shapes=[pltpu.VMEM((FFN_TILE, D_MODEL), F32)],
        compiler_params=_params(1, 2),
        name="ffn",
    )(h, x1, gates, mod_l, g_post.reshape(1, D_MODEL), w1, w3, w2)


def _rope_tables(seq_len):
    half = HEAD_DIM // 2
    nf = half // 2
    inv = ROPE_THETA ** (-jnp.arange(nf, dtype=F32) / nf)
    t = jnp.arange(seq_len)
    row = (t // GRID_W).astype(F32)
    col = (t % GRID_W).astype(F32)
    lane = jnp.arange(PAIR)
    pos = jnp.where(((lane % HEAD_DIM) // half == 0)[None, :], row[:, None], col[:, None])
    ang = pos * inv[lane % nf][None, :]
    first = ((lane % half) < nf)[None, :]
    return jnp.cos(ang), jnp.where(first, -jnp.sin(ang), jnp.sin(ang))


def _constants():
    r = jnp.arange(RW)
    bd = (r[:, None] // R_HEAD == r[None, :] // R_HEAD).astype(BF16)
    t = jnp.arange(TILE)
    same = t[:, None] // CHUNK == t[None, :] // CHUNK
    tri = jnp.stack([same & (t[None, :] <= t[:, None]), same & (t[None, :] >= t[:, None])])
    return {"bd": bd, "tri": tri.astype(BF16)}


def _pad_rows(w2):
    z = jnp.zeros_like(w2[0])
    return jnp.stack([jnp.concatenate([w2[0], z], 0), jnp.concatenate([z, w2[1]], 0)]).astype(BF16)


def kernel(x_prompt, x_sample, cache_k, cache_v, state_rwkv, c, c_ctx, w_mod, b_mod, g_pre_mix, g_post_mix, g_pre_ffn, g_post_ffn, w_in, mu_shift, w_o, attn_sink, rw_w0, rw_w1, rw_w2, rw_a0, rw_a1, rw_a2, rw_g1, rw_g2, rw_k_k, rw_k_a, rw_r_k, rw_ln_g, rw_ln_b, ffn_w1, ffn_w3, ffn_w2, moe_router, moe_w1, moe_w3, moe_w2):
    n_ctx, s_ctx, _ = x_prompt.shape
    n_lat, s_lat, _ = x_sample.shape
    n_past = cache_k.shape[2]
    assert s_ctx == TILE and s_lat % TILE == 0 and (n_ctx * s_ctx) % s_lat == 0
    nt_ctx = n_ctx * s_ctx
    n_ctx_tiles = nt_ctx // TILE
    tiles_per_lat = s_lat // TILE
    mod_map = functools.partial(_mod_row, n_ctx_tiles=n_ctx_tiles, tiles_per_lat=tiles_per_lat)
    ffn_per = FFN_TILE // TILE
    mod_map_ffn = lambda i: mod_map(i * ffn_per)

    x = jnp.concatenate([x_prompt.reshape(nt_ctx, D_MODEL), x_sample.reshape(n_lat * s_lat, D_MODEL)], 0)
    n_cond = 8
    cond = jnp.concatenate([c_ctx[None, :], c, jnp.zeros((n_cond - 1 - n_lat, D_MODEL), F32)], 0)
    mods = _modulation(cond, w_mod, b_mod).reshape(DEPTH, n_cond, 6, D_MODEL)

    consts = _constants()
    cos_t, sin_t = _rope_tables(s_lat)
    w_cat = jnp.concatenate(
        [w_in, rw_w1[:, 0], rw_w1[:, 1], rw_a1[:, 0], rw_a1[:, 1], rw_g1], 2).astype(BF16)
    w_o_b = w_o.astype(BF16)
    dense_w = [w.astype(BF16)[:, None] for w in (ffn_w1, ffn_w3, ffn_w2)]
    moe_w = [w.astype(BF16) for w in (moe_w1, moe_w3, moe_w2)]
    router_pad = jnp.pad(moe_router, ((0, 0), (0, 0), (0, 128 - N_EXPERTS)))
    sink_rows = jnp.broadcast_to(attn_sink[:, :, None], (DEPTH, N_HEADS, 128))
    ck = cache_k.reshape(n_lat, DEPTH, n_past, ATT_KV)
    cv = cache_v.reshape(n_lat, DEPTH, n_past, ATT_KV)

    new_k, new_v, new_s = [], [], []
    for l in range(DEPTH):
        p = {"mu": mu_shift[l].reshape(1, -1), "w0": rw_w0[l], "a0": rw_a0[l],
             "w2": _pad_rows(rw_w2[l]), "a2": _pad_rows(rw_a2[l]),
             "g2": rw_g2[l].astype(BF16), "k_k": rw_k_k[l].reshape(1, -1),
             "k_a": rw_k_a[l].reshape(1, -1), "r_k": rw_r_k[l].reshape(1, -1),
             "ln_g": rw_ln_g[l], "ln_b": rw_ln_b[l],
             "g_post_mix": g_post_mix[l], "g_pre_ffn": g_pre_ffn[l]}

        q, kv, rkv, lora = _inproj(x, mods[l], g_pre_mix[l], w_cat, l, mod_map)
        v_b, al, be, ka, rh, wl, bonus, g = _prep(rkv, lora, p, consts, n_ctx_tiles, tiles_per_lat)
        y0, y1, s_ctx_new = _scan(al, be, ka, rh, v_b, wl, state_rwkv[:, l],
                                  n_ctx_tiles, tiles_per_lat)
        o_ctx = _attn_ctx(q, kv, sink_rows[l], n_ctx, s_ctx)
        o_lat = _attn_lat(q, kv, ck, cv, l, cos_t, sin_t, sink_rows[l], nt_ctx, n_lat, s_lat)

        i = l // 2
        moe = l % 2 == 1
        x1, h, gates = _postmix(x, o_ctx, o_lat, y0, y1, bonus, g, mods[l], p, w_o_b, l, consts,
                                mod_map, router_pad[i] if moe else None, n_ctx_tiles)
        w1, w3, w2 = moe_w if moe else dense_w
        x = _ffn(h, x1, gates, mods[l], g_post_ffn[l], w1, w3, w2, i, mod_map_ffn)

        new_k.append(kv[:nt_ctx, :ATT_KV].reshape(n_ctx, s_ctx, KV_HEADS, HEAD_DIM))
        new_v.append(kv[:nt_ctx, ATT_KV:].reshape(n_ctx, s_ctx, KV_HEADS, HEAD_DIM))
        new_s.append(s_ctx_new)

    y_prompt = x[:nt_ctx].reshape(n_ctx, s_ctx, D_MODEL)
    y_sample = x[nt_ctx:].reshape(n_lat, s_lat, D_MODEL)
    return (y_prompt, y_sample, jnp.stack(new_k, 1), jnp.stack(new_v, 1), jnp.stack(new_s, 1))
```

```python
import functools

import jax
import jax.numpy as jnp
from jax import lax
from jax.experimental import pallas as pl
from jax.experimental.pallas import tpu as pltpu

F32 = jnp.float32
BF16 = jnp.bfloat16

D_MODEL = 1024
DEPTH = 4
GRID_W = 64
N_HEADS = 8
KV_HEADS = 2
HEAD_DIM = 64
WINDOW = 128
ROPE_THETA = 10000.0
R_HEADS = 8
R_HEAD = 64
RW = R_HEADS * R_HEAD
LORA_W = 64
LORA_A = 64
LORA_G = 128
LORA_ALL = 2 * LORA_W + 2 * LORA_A + LORA_G
ATT_Q = N_HEADS * HEAD_DIM
ATT_KV = KV_HEADS * HEAD_DIM
D_IN = ATT_Q + 2 * ATT_KV + 3 * RW
D_FF = 2816
N_EXPERTS = 8
D_FF_EXPERT = 1408
NORM_EPS = 1e-6
GN_EPS = 64e-5
NEG_INF = -1e30

TILE = 256
CHUNK = 64
PAIR = 2 * R_HEAD
N_PAIRS = R_HEADS // 2
FFN_TILE = 512
FF_BLOCK = 1408
MOD_COLS = 1536
VMEM_LIMIT = 48 * 1024 * 1024

_NT = (((1,), (1,)), ((), ()))


def _params(n_parallel, n_arbitrary=0):
    sem = ("parallel",) * n_parallel + ("arbitrary",) * n_arbitrary
    return pltpu.CompilerParams(dimension_semantics=sem, vmem_limit_bytes=VMEM_LIMIT)


def _sigmoid(x):
    return 0.5 * jnp.tanh(0.5 * x) + 0.5


def _mm(a, b):
    return jnp.dot(a.astype(BF16), b.astype(BF16), preferred_element_type=F32)


def _mm_nt(a, b):
    return lax.dot_general(a.astype(BF16), b.astype(BF16), _NT, preferred_element_type=F32)


def _split2(x):
    hi = x.astype(BF16)
    return hi, (x - hi.astype(F32)).astype(BF16)


def _head_sum(x, ones_bf16):
    return jnp.dot(x.astype(BF16), ones_bf16, preferred_element_type=F32)


def _mm_ones_left(ones_bf16, x):
    hi, lo = _split2(x)
    d = functools.partial(jnp.dot, preferred_element_type=F32)
    return d(ones_bf16, hi) + d(ones_bf16, lo)


def _rms(x):
    return x * lax.rsqrt(jnp.mean(x * x, -1, keepdims=True) + NORM_EPS)


def _at_layer(shape, layer):
    zeros = (0,) * len(shape)
    return pl.BlockSpec((None,) + tuple(shape), lambda *_: (layer,) + zeros)


def _mod_row(i, n_ctx_tiles, tiles_per_lat):
    return jnp.where(i < n_ctx_tiles, 0, 1 + (i - n_ctx_tiles) // tiles_per_lat)


def _mod_kernel(cond_ref, w_ref, b_ref, o_ref):
    c = cond_ref[...]
    s = c * _sigmoid(c)
    o_ref[0] = _mm(s, w_ref[0]) + b_ref[0]


def _modulation(cond, w_mod, b_mod):
    n_cond = cond.shape[0]
    n_col = 6 * D_MODEL // MOD_COLS
    return pl.pallas_call(
        _mod_kernel,
        grid=(DEPTH, n_col),
        in_specs=[pl.BlockSpec((n_cond, D_MODEL), lambda l, j: (0, 0)),
                  pl.BlockSpec((1, D_MODEL, MOD_COLS), lambda l, j: (l, 0, j)),
                  pl.BlockSpec((1, 1, MOD_COLS), lambda l, j: (l, 0, j))],
        out_specs=pl.BlockSpec((1, n_cond, MOD_COLS), lambda l, j: (l, 0, j)),
        out_shape=jax.ShapeDtypeStruct((DEPTH, n_cond, 6 * D_MODEL), F32),
        compiler_params=_params(2),
        name="modulation",
    )(cond, w_mod, b_mod.reshape(DEPTH, 1, 6 * D_MODEL))


def _inproj_kernel(x_ref, mod_ref, g_ref, w_ref, q_ref, kv_ref, rkv_ref, lora_ref):
    m = mod_ref[0]
    h = _rms(x_ref[...]) * g_ref[...]
    h = h * (1.0 + m[1:2]) + m[0:1]
    u = jnp.dot(h.astype(BF16), w_ref[...], preferred_element_type=F32)
    q_ref[...] = u[:, :ATT_Q]
    kv_ref[...] = u[:, ATT_Q:ATT_Q + 2 * ATT_KV]
    rkv_ref[...] = u[:, ATT_Q + 2 * ATT_KV:D_IN]
    lora_ref[...] = u[:, D_IN:]


def _inproj(x, mods, g, w_cat, layer, mod_map):
    nt = x.shape[0]
    row = lambda i: (i, 0)
    return pl.pallas_call(
        _inproj_kernel,
        grid=(nt // TILE,),
        in_specs=[pl.BlockSpec((TILE, D_MODEL), row),
                  pl.BlockSpec((None, 1, 6, D_MODEL), lambda i: (layer, mod_map(i), 0, 0)),
                  _at_layer((1, D_MODEL), layer),
                  _at_layer((D_MODEL, D_IN + LORA_ALL), layer)],
        out_specs=[pl.BlockSpec((TILE, ATT_Q), row),
                   pl.BlockSpec((TILE, 2 * ATT_KV), row),
                   pl.BlockSpec((TILE, 3 * RW), row),
                   pl.BlockSpec((TILE, LORA_ALL), row)],
        out_shape=[jax.ShapeDtypeStruct((nt, ATT_Q), F32),
                   jax.ShapeDtypeStruct((nt, 2 * ATT_KV), F32),
                   jax.ShapeDtypeStruct((nt, 3 * RW), F32),
                   jax.ShapeDtypeStruct((nt, LORA_ALL), F32)],
        compiler_params=_params(1),
        name="inproj",
    )(x, mods, g, w_cat)


def _prep_kernel(rkv_ref, prev_ref, next_ref, lora_ref, mu_ref, w0_ref, w2_ref, a0_ref, a2_ref,
                 g2_ref, kk_ref, ka_ref, rk_ref, bd_ref, tri_ref,
                 v_out, al_out, be_out, ka_out, rh_out, wl_out, bonus_out, g_out,
                 *, n_ctx_tiles, tiles_per_lat):
    i = pl.program_id(0)
    j = (i - n_ctx_tiles) % tiles_per_lat
    is_ctx = i < n_ctx_tiles
    first = is_ctx | (j == 0)
    last = is_ctx | (j == tiles_per_lat - 1)

    u = rkv_ref[...]
    row = lax.broadcasted_iota(jnp.int32, (TILE, 1), 0)
    p_row = prev_ref[7:8, :] * (1.0 - first.astype(F32))
    n_row = next_ref[0:1, :] * (1.0 - last.astype(F32))
    prev = jnp.where(row == 0, p_row, pltpu.roll(u, 1, 0))
    nxt = jnp.where(row == TILE - 1, n_row, pltpu.roll(u, TILE - 1, 0))
    us = u + mu_ref[...] * (0.5 * (prev + nxt) - u)
    r = us[:, :RW]
    k = us[:, RW:2 * RW]
    v = us[:, 2 * RW:]

    bd = bd_ref[...]
    kk = k * kk_ref[...]
    kk = kk * lax.rsqrt(_head_sum(kk * kk, bd) + 1e-12)

    lora = lora_ref[...]
    lw_in = jnp.tanh(lora[:, :2 * LORA_W])
    la_in = lora[:, 2 * LORA_W:2 * LORA_W + 2 * LORA_A]
    bonus = jnp.zeros_like(v)
    for d in range(2):
        w_pre = w0_ref[d:d + 1, :] + _mm(lw_in, w2_ref[d])
        lw = -jnp.exp(-0.5) * _sigmoid(w_pre)
        a = _sigmoid(a0_ref[d:d + 1, :] + _mm(la_in, a2_ref[d]))
        kd = k * (1.0 + (a - 1.0) * ka_ref[...])
        bonus = bonus + _head_sum(r * kd * rk_ref[...], bd) * v
        cw = _mm_ones_left(tri_ref[d], lw)
        e_neg = jnp.exp(-cw)
        al_out[d] = (kk * jnp.exp(cw - lw)).astype(BF16)
        be_out[d] = (kk * a * e_neg).astype(BF16)
        ka_out[d] = (kd * e_neg).astype(BF16)
        rh_out[d] = (r * jnp.exp(cw)).astype(BF16)
        for c in range(TILE // CHUNK):
            end = c * CHUNK + (CHUNK - 1 if d == 0 else 0)
            wl_out[0, 4 * d + c:4 * d + c + 1, :] = jnp.exp(cw[end:end + 1, :])
    v_out[...] = v.astype(BF16)
    bonus_out[...] = bonus
    g_out[...] = _mm(_sigmoid(lora[:, 2 * LORA_W + 2 * LORA_A:]), g2_ref[...])


def _prep(rkv, lora, p, layer, consts, n_ctx_tiles, tiles_per_lat):
    nt = rkv.shape[0]
    n_tiles = nt // TILE
    row = lambda i: (i, 0)
    drow = lambda i: (0, i, 0)
    c2 = lambda i: (0, 0)
    c3 = lambda i: (0, 0, 0)
    hb = TILE // 8
    kern = functools.partial(_prep_kernel, n_ctx_tiles=n_ctx_tiles, tiles_per_lat=tiles_per_lat)
    dshape = jax.ShapeDtypeStruct((2, nt, RW), BF16)
    return pl.pallas_call(
        kern,
        grid=(n_tiles,),
        in_specs=[pl.BlockSpec((TILE, 3 * RW), row),
                  pl.BlockSpec((8, 3 * RW), lambda i: (jnp.maximum(i * hb - 1, 0), 0)),
                  pl.BlockSpec((8, 3 * RW), lambda i: (jnp.minimum((i + 1) * hb, nt // 8 - 1), 0)),
                  pl.BlockSpec((TILE, LORA_ALL), row),
                  _at_layer((1, 3 * RW), layer),
                  _at_layer((2, RW), layer),
                  _at_layer((2, 2 * LORA_W, RW), layer),
                  _at_layer((2, RW), layer),
                  _at_layer((2, 2 * LORA_A, RW), layer),
                  _at_layer((LORA_G, RW), layer),
                  _at_layer((1, RW), layer),
                  _at_layer((1, RW), layer),
                  _at_layer((1, RW), layer),
                  pl.BlockSpec((RW, RW), c2),
                  pl.BlockSpec((2, TILE, TILE), c3)],
        out_specs=[pl.BlockSpec((TILE, RW), row),
                   pl.BlockSpec((2, TILE, RW), drow),
                   pl.BlockSpec((2, TILE, RW), drow),
                   pl.BlockSpec((2, TILE, RW), drow),
                   pl.BlockSpec((2, TILE, RW), drow),
                   pl.BlockSpec((1, 8, RW), lambda i: (i, 0, 0)),
                   pl.BlockSpec((TILE, RW), row),
                   pl.BlockSpec((TILE, RW), row)],
        out_shape=[jax.ShapeDtypeStruct((nt, RW), BF16), dshape, dshape, dshape, dshape,
                   jax.ShapeDtypeStruct((n_tiles, 8, RW), F32),
                   jax.ShapeDtypeStruct((nt, RW), F32),
                   jax.ShapeDtypeStruct((nt, RW), F32)],
        compiler_params=_params(1),
        name="rwkv_prep",
    )(rkv, rkv, rkv, lora, p["mu"], p["w0"], p["w2"], p["a0"], p["a2"], p["g2"],
      p["k_k"], p["k_a"], p["r_k"], consts["bd"], consts["tri"])


def _scan_kernel(al0, be0, ka0, rh0, v0, wl0, al1, be1, ka1, rh1, v1, wl1, s0_ref,
                 y0_ref, y1_ref, st_ref, s_scr, *, n_ctx_tiles, tiles_per_lat):
    i = pl.program_id(0)
    is_ctx = i < n_ctx_tiles
    seq_start = (i - n_ctx_tiles) % tiles_per_lat == 0

    @pl.when(is_ctx)
    def _():
        s_scr[...] = jnp.zeros_like(s_scr)

    @pl.when(jnp.logical_not(is_ctx) & seq_start)
    def _():
        s_scr[...] = s0_ref[...]

    lo = lax.broadcasted_iota(jnp.int32, (CHUNK, PAIR), 1) < R_HEAD
    ri = lax.broadcasted_iota(jnp.int32, (PAIR, PAIR), 0)
    ci = lax.broadcasted_iota(jnp.int32, (PAIR, PAIR), 1)
    same = (ri // CHUNK) == (ci // CHUNK)
    rt = ri % CHUNK
    ct = ci % CHUNK
    before = [(ct < rt) & same, (ct > rt) & same]
    upto = [(ct <= rt) & same, (ct >= rt) & same]
    n_sq = CHUNK.bit_length() - 2
    n_chunks = TILE // CHUNK
    refs = [(al0, be0, ka0, rh0, v0, wl0, y0_ref), (al1, be1, ka1, rh1, v1, wl1, y1_ref)]

    def expand(x):
        zero = jnp.zeros_like(x)
        return jnp.concatenate([jnp.where(lo, x, zero), jnp.where(lo, zero, x)], 0)

    def chunk(c, carry):
        chains = []
        for d in range(2):
            al, be, ka, rh, v, wl, y_ref = refs[d]
            ce = c if d == 0 else n_chunks - 1 - c
            rows = pl.ds(pl.multiple_of(ce * CHUNK, CHUNK), CHUNK)
            wl_all = wl[pl.ds(n_chunks * d + ce, 1), :]
            for p in range(N_PAIRS):
                lanes = slice(PAIR * p, PAIR * (p + 1))
                chains.append(dict(
                    d=d, p=p, rows=rows, lanes=lanes, y_ref=y_ref, wl=wl_all[:, lanes],
                    a=expand(al[rows, lanes]), b=expand(be[rows, lanes]),
                    k=expand(ka[rows, lanes]), r=expand(rh[rows, lanes]),
                    v=expand(v[rows, lanes]), s=s_scr[d, p]))

        zero = jnp.zeros((PAIR, PAIR), F32)
        for ch in chains:
            ar = jnp.concatenate([ch["a"], ch["r"]], 0)
            bks = jnp.concatenate([ch["b"], ch["k"], ch["s"].astype(BF16)], 0)
            ch["gram"] = lax.dot_general(ar, bks, _NT, preferred_element_type=F32)
        for ch in chains:
            g, d = ch["gram"], ch["d"]
            ch["m_ab"] = jnp.where(before[d], g[:PAIR, :PAIR], zero)
            ch["m_rb"] = jnp.where(upto[d], g[PAIR:, :PAIR], zero)
            m_ak = jnp.where(before[d], g[:PAIR, PAIR:2 * PAIR], zero)
            m_rk = jnp.where(upto[d], g[PAIR:, PAIR:2 * PAIR], zero)
            ch["mv"] = _mm(jnp.concatenate([m_ak, m_rk], 0), ch["v"])

        for ch in chains:
            ch["x"] = ch["gram"][:PAIR, 2 * PAIR:] + ch["mv"][:PAIR]
            ch["pw"] = ch["m_ab"]
        for lvl in range(n_sq + 1):
            for ch in chains:
                if lvl < n_sq:
                    px = _mm(ch["pw"], jnp.concatenate([ch["pw"], ch["x"]], 1))
                    ch["pw"] = px[:, :PAIR]
                    step = px[:, PAIR:]
                else:
                    step = _mm(ch["pw"], ch["x"])
                ch["x"] = ch["x"] - step if lvl == 0 else ch["x"] + step

        for ch in chains:
            z = -ch["x"]
            y2 = ch["gram"][PAIR:, 2 * PAIR:] + ch["mv"][PAIR:] + _mm(ch["m_rb"], z)
            ch["y"] = y2[:CHUNK] + y2[CHUNK:]
            ch["zv_t"] = jnp.concatenate([z, ch["v"].astype(F32)], 0).T
        for ch in chains:
            bk = jnp.concatenate([ch["b"], ch["k"]], 0)
            ch["s_new"] = (ch["s"] + _mm(ch["zv_t"], bk)) * ch["wl"]
        for ch in chains:
            ch["y_ref"][ch["rows"], ch["lanes"]] = ch["y"]
            s_scr[ch["d"], ch["p"]] = ch["s_new"]
        return carry

    lax.fori_loop(0, n_chunks, chunk, 0)
    for d in range(2):
        for p in range(N_PAIRS):
            s = s_scr[d, p]
            st_ref[d, 2 * p] = s[:R_HEAD, :R_HEAD]
            st_ref[d, 2 * p + 1] = s[R_HEAD:, R_HEAD:]


def _expand_state(s0):
    n = s0.shape[0]
    s = s0.reshape(n, 2, N_PAIRS, 2, R_HEAD, R_HEAD)
    z = jnp.zeros_like(s[:, :, :, 0])
    top = jnp.concatenate([s[:, :, :, 0], z], -1)
    bot = jnp.concatenate([z, s[:, :, :, 1]], -1)
    return jnp.concatenate([top, bot], -2)


def _scan(al, be, ka, rh, v, wl, s0_lat, n_ctx_tiles, tiles_per_lat):
    nt = v.shape[0]
    n_tiles = nt // TILE
    n_lat = s0_lat.shape[0]

    def mirror(i):
        j = (i - n_ctx_tiles) % tiles_per_lat
        return jnp.where(i < n_ctx_tiles, i, i - j + (tiles_per_lat - 1 - j))

    lat_seq = lambda i: jnp.clip((i - n_ctx_tiles) // tiles_per_lat, 0, n_lat - 1)
    fwd3 = pl.BlockSpec((None, TILE, RW), lambda i: (0, i, 0))
    bwd3 = pl.BlockSpec((None, TILE, RW), lambda i: (1, mirror(i), 0))
    fwd2 = pl.BlockSpec((TILE, RW), lambda i: (i, 0))
    bwd2 = pl.BlockSpec((TILE, RW), lambda i: (mirror(i), 0))
    wl_f = pl.BlockSpec((None, 8, RW), lambda i: (i, 0, 0))
    wl_b = pl.BlockSpec((None, 8, RW), lambda i: (mirror(i), 0, 0))
    state_in = (None, 2, N_PAIRS, PAIR, PAIR)
    state_out = (None, 2, R_HEADS, R_HEAD, R_HEAD)
    kern = functools.partial(_scan_kernel, n_ctx_tiles=n_ctx_tiles, tiles_per_lat=tiles_per_lat)
    y0, y1, st = pl.pallas_call(
        kern,
        grid=(n_tiles,),
        in_specs=[fwd3, fwd3, fwd3, fwd3, fwd2, wl_f, bwd3, bwd3, bwd3, bwd3, bwd2, wl_b,
                  pl.BlockSpec(state_in, lambda i: (lat_seq(i), 0, 0, 0, 0))],
        out_specs=[fwd2, bwd2,
                   pl.BlockSpec(state_out, lambda i: (jnp.minimum(i, n_ctx_tiles), 0, 0, 0, 0))],
        out_shape=[jax.ShapeDtypeStruct((nt, RW), F32), jax.ShapeDtypeStruct((nt, RW), F32),
                   jax.ShapeDtypeStruct((n_ctx_tiles + 1, 2, R_HEADS, R_HEAD, R_HEAD), F32)],
        scratch_shapes=[pltpu.VMEM((2, N_PAIRS, PAIR, PAIR), F32)],
        compiler_params=_params(0, 1),
        name="wkv_scan",
    )(al, be, ka, rh, v, wl, al, be, ka, rh, v, wl, _expand_state(s0_lat))
    return y0, y1, st[:n_ctx_tiles]


def _dup_halves(x, lo):
    xr = pltpu.roll(x, HEAD_DIM, 1)
    return jnp.where(lo, x, xr), jnp.where(lo, xr, x)


def _stack_heads(qp, lo):
    zero = jnp.zeros_like(qp)
    return jnp.concatenate([jnp.where(lo, qp, zero), jnp.where(lo, zero, qp)], 0)


def _sink_col(sink_ref, p, rows):
    r = lax.broadcasted_iota(jnp.int32, (2 * rows, 1), 0)
    return jnp.where(r < rows, sink_ref[2 * p:2 * p + 1, 0:1], sink_ref[2 * p + 1:2 * p + 2, 0:1])


def _attn_ctx_kernel(q_ref, kv_ref, sink_ref, o_ref):
    rows = q_ref.shape[0]
    lo = lax.broadcasted_iota(jnp.int32, (rows, PAIR), 1) < HEAD_DIM
    k_dup = _dup_halves(kv_ref[:, :ATT_KV], lo)
    v_dup = _dup_halves(kv_ref[:, ATT_KV:], lo)
    scale = HEAD_DIM ** -0.5
    pairs = range(N_HEADS // 2)
    kv_head = [(2 * p) // (N_HEADS // KV_HEADS) for p in pairs]
    s = [_mm_nt(_stack_heads(q_ref[:, PAIR * p:PAIR * (p + 1)] * scale, lo), k_dup[kv_head[p]])
         for p in pairs]
    sk = [_sink_col(sink_ref, p, rows) for p in pairs]
    m = [jnp.maximum(jnp.max(s[p], -1, keepdims=True), sk[p]) for p in pairs]
    e = [jnp.exp(s[p] - m[p]) for p in pairs]
    den = [jnp.sum(e[p], -1, keepdims=True) + jnp.exp(sk[p] - m[p]) for p in pairs]
    o2 = [_mm(e[p] / den[p], v_dup[kv_head[p]]) for p in pairs]
    for p in pairs:
        o_ref[:, PAIR * p:PAIR * (p + 1)] = jnp.where(lo, o2[p][:rows], o2[p][rows:])


def _attn_ctx(q, kv, sink_rows, layer, n_seq, seq_len):
    return pl.pallas_call(
        _attn_ctx_kernel,
        grid=(n_seq,),
        in_specs=[pl.BlockSpec((seq_len, ATT_Q), lambda b: (b, 0)),
                  pl.BlockSpec((seq_len, 2 * ATT_KV), lambda b: (b, 0)),
                  _at_layer((N_HEADS, 128), layer)],
        out_specs=pl.BlockSpec((seq_len, ATT_Q), lambda b: (b, 0)),
        out_shape=jax.ShapeDtypeStruct((n_seq * seq_len, ATT_Q), F32),
        compiler_params=_params(1),
        name="attn_ctx",
    )(q, kv, sink_rows)


def _rope(x, cos, sgn_sin, first16):
    sw = jnp.where(first16, pltpu.roll(x, PAIR - 16, 1), pltpu.roll(x, 16, 1))
    return x * cos + sw * sgn_sin


def _attn_lat_kernel(q_ref, kv_ref, ck_ref, cv_ref, cos_ref, sin_ref, sink_ref, o_ref,
                     kd_scr, vd_scr, ckd_scr, cvd_scr, *, seq_len):
    i = pl.program_id(1)
    qb = q_ref.shape[0]
    n_blk = seq_len // qb

    @pl.when(i == 0)
    def _():
        lo_s = lax.broadcasted_iota(jnp.int32, (seq_len, PAIR), 1) < HEAD_DIM
        f16_s = (lax.broadcasted_iota(jnp.int32, (seq_len, PAIR), 1) % 32) < 16
        k_r = _rope(kv_ref[:, :ATT_KV], cos_ref[...], sin_ref[...], f16_s)
        k0, k1 = _dup_halves(k_r, lo_s)
        v0, v1 = _dup_halves(kv_ref[:, ATT_KV:], lo_s)
        kd_scr[0] = k0.astype(BF16)
        kd_scr[1] = k1.astype(BF16)
        vd_scr[0] = v0.astype(BF16)
        vd_scr[1] = v1.astype(BF16)
        lo_c = lax.broadcasted_iota(jnp.int32, ck_ref.shape[1:], 1) < HEAD_DIM
        c0, c1 = _dup_halves(ck_ref[0], lo_c)
        ckd_scr[0] = c0.astype(BF16)
        ckd_scr[1] = c1.astype(BF16)
        c0, c1 = _dup_halves(cv_ref[0], lo_c)
        cvd_scr[0] = c0.astype(BF16)
        cvd_scr[1] = c1.astype(BF16)

    lo = lax.broadcasted_iota(jnp.int32, (qb, PAIR), 1) < HEAD_DIM
    f16 = (lax.broadcasted_iota(jnp.int32, (qb, PAIR), 1) % 32) < 16
    q0 = pl.multiple_of(i * qb, qb)
    cos = cos_ref[pl.ds(q0, qb), :]
    sin = sin_ref[pl.ds(q0, qb), :]
    qpos = i * qb + lax.broadcasted_iota(jnp.int32, (2 * qb, qb), 0) % qb
    kcol = lax.broadcasted_iota(jnp.int32, (2 * qb, qb), 1)
    scale = HEAD_DIM ** -0.5
    n_past = ck_ref.shape[1]

    pairs = range(N_HEADS // 2)
    kv_head = [(2 * p) // (N_HEADS // KV_HEADS) for p in pairs]
    win = []
    for jj in range(3):
        j = i - 1 + jj
        jc = jnp.clip(j, 0, n_blk - 1)
        kpos = jnp.where(j == jc, jc * qb, -4 * seq_len) + kcol
        win.append((pl.ds(pl.multiple_of(jc * qb, qb), qb), jnp.abs(qpos - kpos) <= WINDOW))

    qs = [_stack_heads(_rope(q_ref[:, PAIR * p:PAIR * (p + 1)], cos, sin, f16) * scale,
                       lo).astype(BF16) for p in pairs]
    s = []
    for p in pairs:
        g = kv_head[p]
        parts = [_mm_nt(qs[p], ckd_scr[g])]
        for rows_k, ok in win:
            parts.append(jnp.where(ok, _mm_nt(qs[p], kd_scr[g, rows_k, :]), NEG_INF))
        s.append(jnp.concatenate(parts, 1))
    sk = [_sink_col(sink_ref, p, qb) for p in pairs]
    m = [jnp.maximum(jnp.max(s[p], -1, keepdims=True), sk[p]) for p in pairs]
    e = [jnp.exp(s[p] - m[p]) for p in pairs]
    den = [jnp.sum(e[p], -1, keepdims=True) + jnp.exp(sk[p] - m[p]) for p in pairs]
    o2 = []
    for p in pairs:
        g = kv_head[p]
        pr = (e[p] / den[p]).astype(BF16)
        acc = jnp.dot(pr[:, :n_past], cvd_scr[g], preferred_element_type=F32)
        for jj, (rows_k, _) in enumerate(win):
            c0 = n_past + jj * qb
            acc = acc + jnp.dot(pr[:, c0:c0 + qb], vd_scr[g, rows_k, :],
                                preferred_element_type=F32)
        o2.append(acc)
    for p in pairs:
        o_ref[:, PAIR * p:PAIR * (p + 1)] = jnp.where(lo, o2[p][:qb], o2[p][qb:])


def _attn_lat(q, kv, ck, cv, layer, cos_t, sin_t, sink_rows, row0, n_seq, seq_len):
    qb = WINDOW
    n_blk = seq_len // qb
    blk0 = row0 // qb
    seq0 = row0 // seq_len
    n_past = ck.shape[2]
    kern = functools.partial(_attn_lat_kernel, seq_len=seq_len)
    return pl.pallas_call(
        kern,
        grid=(n_seq, n_blk),
        in_specs=[pl.BlockSpec((qb, ATT_Q), lambda b, i: (blk0 + b * n_blk + i, 0)),
                  pl.BlockSpec((seq_len, 2 * ATT_KV), lambda b, i: (seq0 + b, 0)),
                  pl.BlockSpec((1, None, n_past, ATT_KV), lambda b, i: (b, layer, 0, 0)),
                  pl.BlockSpec((1, None, n_past, ATT_KV), lambda b, i: (b, layer, 0, 0)),
                  pl.BlockSpec((seq_len, PAIR), lambda b, i: (0, 0)),
                  pl.BlockSpec((seq_len, PAIR), lambda b, i: (0, 0)),
                  _at_layer((N_HEADS, 128), layer)],
        out_specs=pl.BlockSpec((qb, ATT_Q), lambda b, i: (b * n_blk + i, 0)),
        out_shape=jax.ShapeDtypeStruct((n_seq * seq_len, ATT_Q), F32),
        scratch_shapes=[pltpu.VMEM((2, seq_len, PAIR), BF16), pltpu.VMEM((2, seq_len, PAIR), BF16),
                        pltpu.VMEM((2, n_past, PAIR), BF16), pltpu.VMEM((2, n_past, PAIR), BF16)],
        compiler_params=_params(1, 1),
        name="attn_lat",
    )(q, kv, ck, cv, cos_t, sin_t, sink_rows)


def _postmix_kernel(x_ref, octx_ref, olat_ref, y0_ref, y1_ref, bonus_ref, g_ref, mod_ref,
                    gpost_ref, gpre_ref, lng_ref, lnb_ref, wo_ref, bd_ref, router_ref,
                    x1_ref, h_ref, gates_ref, *, with_router, n_ctx_tiles):
    m = mod_ref[0]
    bd = bd_ref[...]
    tile = lax.broadcasted_iota(jnp.int32, octx_ref.shape, 0) * 0 + pl.program_id(0)
    o_att = jnp.where(tile < n_ctx_tiles, octx_ref[...], olat_ref[...])
    y = y0_ref[...] + y1_ref[...]
    inv_n = 1.0 / R_HEAD
    mu = _head_sum(y, bd) * inv_n
    yc = y - mu
    var = _head_sum(yc * yc, bd) * inv_n
    yn = yc * lax.rsqrt(var + GN_EPS)
    o_rw = (yn * lng_ref[...] + lnb_ref[...] + bonus_ref[...]) * g_ref[...]
    o = (jnp.dot(o_att.astype(BF16), wo_ref[:ATT_Q, :], preferred_element_type=F32)
         + jnp.dot(o_rw.astype(BF16), wo_ref[ATT_Q:, :], preferred_element_type=F32))
    x1 = x_ref[...] + m[2:3] * (_rms(o) * gpost_ref[...])
    x1_ref[...] = x1
    h = (_rms(x1) * gpre_ref[...]) * (1.0 + m[4:5]) + m[3:4]
    h_ref[...] = h.astype(BF16)
    if with_router:
        h_hi, h_lo = _split2(h)
        r_hi, r_lo = _split2(router_ref[...])
        d = functools.partial(jnp.dot, preferred_element_type=F32)
        logits = d(h_hi, r_hi) + (d(h_hi, r_lo) + d(h_lo, r_hi))
        lane = lax.broadcasted_iota(jnp.int32, logits.shape, 1)
        valid = lane < N_EXPERTS
        logits = jnp.where(valid, logits, NEG_INF)
        e = jnp.exp(logits - jnp.max(logits, -1, keepdims=True))
        probs = e / jnp.sum(e, -1, keepdims=True)
        lane_f = lane.astype(F32)
        p1 = jnp.max(probs, -1, keepdims=True)
        i1 = jnp.min(jnp.where(probs == p1, lane_f, 1e9), -1, keepdims=True)
        rest = jnp.where(lane_f == i1, -1.0, probs)
        p2 = jnp.max(rest, -1, keepdims=True)
        i2 = jnp.min(jnp.where(rest == p2, lane_f, 1e9), -1, keepdims=True)
        sel = (lane_f == i1) | (lane_f == i2)
        gates_ref[...] = jnp.where(sel, probs / (p1 + p2), 0.0)
    else:
        gates_ref[...] = jnp.ones(gates_ref.shape, F32)


def _postmix(x, o_ctx, o_lat, y0, y1, bonus, g, mods, p, w_o, layer, consts, mod_map,
             router_pad, router_layer, n_ctx_tiles):
    nt = x.shape[0]
    row = lambda i: (i, 0)
    c2 = lambda i: (0, 0)
    with_router = router_layer is not None
    kern = functools.partial(_postmix_kernel, with_router=with_router, n_ctx_tiles=n_ctx_tiles)
    n_lat_tiles = nt // TILE - n_ctx_tiles
    return pl.pallas_call(
        kern,
        grid=(nt // TILE,),
        in_specs=[pl.BlockSpec((TILE, D_MODEL), row),
                  pl.BlockSpec((TILE, ATT_Q), lambda i: (jnp.minimum(i, n_ctx_tiles - 1), 0)),
                  pl.BlockSpec((TILE, ATT_Q),
                               lambda i: (jnp.clip(i - n_ctx_tiles, 0, n_lat_tiles - 1), 0)),
                  pl.BlockSpec((TILE, RW), row),
                  pl.BlockSpec((TILE, RW), row),
                  pl.BlockSpec((TILE, RW), row),
                  pl.BlockSpec((TILE, RW), row),
                  pl.BlockSpec((None, 1, 6, D_MODEL), lambda i: (layer, mod_map(i), 0, 0)),
                  _at_layer((1, D_MODEL), layer),
                  _at_layer((1, D_MODEL), layer),
                  _at_layer((1, RW), layer),
                  _at_layer((1, RW), layer),
                  _at_layer((ATT_Q + RW, D_MODEL), layer),
                  pl.BlockSpec((RW, RW), c2),
                  _at_layer((D_MODEL, 128), router_layer or 0)],
        out_specs=[pl.BlockSpec((TILE, D_MODEL), row),
                   pl.BlockSpec((TILE, D_MODEL), row),
                   pl.BlockSpec((TILE, 128), row)],
        out_shape=[jax.ShapeDtypeStruct((nt, D_MODEL), F32),
                   jax.ShapeDtypeStruct((nt, D_MODEL), BF16),
                   jax.ShapeDtypeStruct((nt, 128), F32)],
        compiler_params=_params(1),
        name="postmix",
    )(x, o_ctx, o_lat, y0, y1, bonus, g, mods, p["g_post_mix"], p["g_pre_ffn"],
      p["ln_g"], p["ln_b"], w_o, consts["bd"], router_pad)


def _ffn_kernel(h_ref, x1_ref, gates_ref, mod_ref, gpost_ref, w1_ref, w3_ref, w2_ref, o_ref,
                acc_ref):
    e = pl.program_id(1)
    f = pl.program_id(2)
    first = (e == 0) & (f == 0)
    last = (e == pl.num_programs(1) - 1) & (f == pl.num_programs(2) - 1)

    @pl.when(first)
    def _():
        acc_ref[...] = jnp.zeros_like(acc_ref)

    h = h_ref[...]
    a = jnp.dot(h, w1_ref[0], preferred_element_type=F32)
    b = jnp.dot(h, w3_ref[0], preferred_element_type=F32)
    t = (a * _sigmoid(a)) * b
    lane = lax.broadcasted_iota(jnp.int32, gates_ref.shape, 1)
    gate = jnp.sum(jnp.where(lane == e, gates_ref[...], 0.0), -1, keepdims=True)
    acc_ref[...] += gate * jnp.dot(t.astype(BF16), w2_ref[0], preferred_element_type=F32)

    @pl.when(last)
    def _():
        m = mod_ref[0]
        o_ref[...] = x1_ref[...] + m[5:6] * (_rms(acc_ref[...]) * gpost_ref[...])


def _ffn(h, x1, gates, mods, g_post, w1, w3, w2, layer, w_layer, mod_map_ffn):
    nt = h.shape[0]
    _, n_e, _, d_ff = w1.shape
    n_f = d_ff // FF_BLOCK
    row = lambda i, e, f: (i, 0)
    return pl.pallas_call(
        _ffn_kernel,
        grid=(nt // FFN_TILE, n_e, n_f),
        in_specs=[pl.BlockSpec((FFN_TILE, D_MODEL), row),
                  pl.BlockSpec((FFN_TILE, D_MODEL), row),
                  pl.BlockSpec((FFN_TILE, 128), row),
                  pl.BlockSpec((None, 1, 6, D_MODEL),
                               lambda i, e, f: (layer, mod_map_ffn(i), 0, 0)),
                  _at_layer((1, D_MODEL), layer),
                  pl.BlockSpec((None, 1, D_MODEL, FF_BLOCK), lambda i, e, f: (w_layer, e, 0, f)),
                  pl.BlockSpec((None, 1, D_MODEL, FF_BLOCK), lambda i, e, f: (w_layer, e, 0, f)),
                  pl.BlockSpec((None, 1, FF_BLOCK, D_MODEL), lambda i, e, f: (w_layer, e, f, 0))],
        out_specs=pl.BlockSpec((FFN_TILE, D_MODEL), row),
        out_shape=jax.ShapeDtypeStruct((nt, D_MODEL), F32),
        scratch_shapes=[pltpu.VMEM((FFN_TILE, D_MODEL), F32)],
        compiler_params=_params(1, 2),
        name="ffn",
    )(h, x1, gates, mods, g_post, w1, w3, w2)


def _rope_tables(seq_len):
    half = HEAD_DIM // 2
    nf = half // 2
    inv = ROPE_THETA ** (-jnp.arange(nf, dtype=F32) / nf)
    t = jnp.arange(seq_len)
    row = (t // GRID_W).astype(F32)
    col = (t % GRID_W).astype(F32)
    lane = jnp.arange(PAIR)
    pos = jnp.where(((lane % HEAD_DIM) // half == 0)[None, :], row[:, None], col[:, None])
    ang = pos * inv[lane % nf][None, :]
    first = ((lane % half) < nf)[None, :]
    return jnp.cos(ang), jnp.where(first, -jnp.sin(ang), jnp.sin(ang))


def _constants():
    r = jnp.arange(RW)
    bd = (r[:, None] // R_HEAD == r[None, :] // R_HEAD).astype(BF16)
    t = jnp.arange(TILE)
    same = t[:, None] // CHUNK == t[None, :] // CHUNK
    tri = jnp.stack([same & (t[None, :] <= t[:, None]), same & (t[None, :] >= t[:, None])])
    return {"bd": bd, "tri": tri.astype(BF16)}


def _pad_rows(w2):
    z = jnp.zeros_like(w2[:, 0])
    return jnp.stack([jnp.concatenate([w2[:, 0], z], 1), jnp.concatenate([z, w2[:, 1]], 1)],
                     1).astype(BF16)


def kernel(x_prompt, x_sample, cache_k, cache_v, state_rwkv, c, c_ctx, w_mod, b_mod, g_pre_mix, g_post_mix, g_pre_ffn, g_post_ffn, w_in, mu_shift, w_o, attn_sink, rw_w0, rw_w1, rw_w2, rw_a0, rw_a1, rw_a2, rw_g1, rw_g2, rw_k_k, rw_k_a, rw_r_k, rw_ln_g, rw_ln_b, ffn_w1, ffn_w3, ffn_w2, moe_router, moe_w1, moe_w3, moe_w2):
    n_ctx, s_ctx, _ = x_prompt.shape
    n_lat, s_lat, _ = x_sample.shape
    n_past = cache_k.shape[2]
    assert s_ctx == TILE and s_lat % TILE == 0 and (n_ctx * s_ctx) % s_lat == 0
    nt_ctx = n_ctx * s_ctx
    n_ctx_tiles = nt_ctx // TILE
    tiles_per_lat = s_lat // TILE
    mod_map = functools.partial(_mod_row, n_ctx_tiles=n_ctx_tiles, tiles_per_lat=tiles_per_lat)
    ffn_per = FFN_TILE // TILE
    mod_map_ffn = lambda i: mod_map(i * ffn_per)

    x = jnp.concatenate([x_prompt.reshape(nt_ctx, D_MODEL), x_sample.reshape(n_lat * s_lat, D_MODEL)], 0)
    n_cond = 8
    cond = jnp.concatenate([c_ctx[None, :], c, jnp.zeros((n_cond - 1 - n_lat, D_MODEL), F32)], 0)
    mods = _modulation(cond, w_mod, b_mod).reshape(DEPTH, n_cond, 6, D_MODEL)

    consts = _constants()
    cos_t, sin_t = _rope_tables(s_lat)
    w_cat = jnp.concatenate(
        [w_in, rw_w1[:, 0], rw_w1[:, 1], rw_a1[:, 0], rw_a1[:, 1], rw_g1], 2).astype(BF16)
    w_o_b = w_o.astype(BF16)
    dense_w = [w.astype(BF16)[:, None] for w in (ffn_w1, ffn_w3, ffn_w2)]
    moe_w = [w.astype(BF16) for w in (moe_w1, moe_w3, moe_w2)]
    router_pad = jnp.pad(moe_router, ((0, 0), (0, 0), (0, 128 - N_EXPERTS)))
    sink_rows = jnp.broadcast_to(attn_sink[:, :, None], (DEPTH, N_HEADS, 128))
    ck = cache_k.reshape(n_lat, DEPTH, n_past, ATT_KV)
    cv = cache_v.reshape(n_lat, DEPTH, n_past, ATT_KV)

    vecs = lambda a: a.reshape(DEPTH, 1, -1)
    p = {"mu": vecs(mu_shift), "w0": rw_w0, "a0": rw_a0, "w2": _pad_rows(rw_w2),
         "a2": _pad_rows(rw_a2), "g2": rw_g2.astype(BF16), "k_k": vecs(rw_k_k),
         "k_a": vecs(rw_k_a), "r_k": vecs(rw_r_k), "ln_g": vecs(rw_ln_g), "ln_b": vecs(rw_ln_b),
         "g_post_mix": vecs(g_post_mix), "g_pre_ffn": vecs(g_pre_ffn)}
    g_pre_mix_v = vecs(g_pre_mix)
    g_post_ffn_v = vecs(g_post_ffn)

    new_k, new_v, new_s = [], [], []
    for l in range(DEPTH):
        q, kv, rkv, lora = _inproj(x, mods, g_pre_mix_v, w_cat, l, mod_map)
        v_b, al, be, ka, rh, wl, bonus, g = _prep(rkv, lora, p, l, consts, n_ctx_tiles,
                                                  tiles_per_lat)
        y0, y1, s_ctx_new = _scan(al, be, ka, rh, v_b, wl, state_rwkv[:, l],
                                  n_ctx_tiles, tiles_per_lat)
        o_ctx = _attn_ctx(q, kv, sink_rows, l, n_ctx, s_ctx)
        o_lat = _attn_lat(q, kv, ck, cv, l, cos_t, sin_t, sink_rows, nt_ctx, n_lat, s_lat)

        i = l // 2
        moe = l % 2 == 1
        x1, h, gates = _postmix(x, o_ctx, o_lat, y0, y1, bonus, g, mods, p, w_o_b, l, consts,
                                mod_map, router_pad, i if moe else None, n_ctx_tiles)
        w1, w3, w2 = moe_w if moe else dense_w
        x = _ffn(h, x1, gates, mods, g_post_ffn_v, w1, w3, w2, l, i, mod_map_ffn)

        new_k.append(kv[:nt_ctx, :ATT_KV].reshape(n_ctx, s_ctx, KV_HEADS, HEAD_DIM))
        new_v.append(kv[:nt_ctx, ATT_KV:].reshape(n_ctx, s_ctx, KV_HEADS, HEAD_DIM))
        new_s.append(s_ctx_new)

    y_prompt = x[:nt_ctx].reshape(n_ctx, s_ctx, D_MODEL)
    y_sample = x[nt_ctx:].reshape(n_lat, s_lat, D_MODEL)
    return (y_prompt, y_sample, jnp.stack(new_k, 1), jnp.stack(new_v, 1), jnp.stack(new_s, 1))
```

```python
import functools

import jax
import jax.numpy as jnp
from jax import lax
from jax.experimental import pallas as pl
from jax.experimental.pallas import tpu as pltpu

F32 = jnp.float32
BF16 = jnp.bfloat16

D_MODEL = 1024
DEPTH = 4
GRID_W = 64
N_HEADS = 8
KV_HEADS = 2
HEAD_DIM = 64
WINDOW = 128
ROPE_THETA = 10000.0
R_HEADS = 8
R_HEAD = 64
RW = R_HEADS * R_HEAD
LORA_W = 64
LORA_A = 64
LORA_G = 128
LORA_ALL = 2 * LORA_W + 2 * LORA_A + LORA_G
ATT_Q = N_HEADS * HEAD_DIM
ATT_KV = KV_HEADS * HEAD_DIM
D_IN = ATT_Q + 2 * ATT_KV + 3 * RW
D_FF = 2816
N_EXPERTS = 8
D_FF_EXPERT = 1408
NORM_EPS = 1e-6
GN_EPS = 64e-5
NEG_INF = -1e30

TILE = 256
CHUNK = 64
PAIR = 2 * R_HEAD
N_PAIRS = R_HEADS // 2
FFN_TILE = 512
FF_BLOCK = 1408
MOD_COLS = 1536
VMEM_LIMIT = 48 * 1024 * 1024

_NT = (((1,), (1,)), ((), ()))


def _params(n_parallel, n_arbitrary=0, vmem_limit=VMEM_LIMIT):
    sem = ("parallel",) * n_parallel + ("arbitrary",) * n_arbitrary
    return pltpu.CompilerParams(dimension_semantics=sem, vmem_limit_bytes=vmem_limit)


def _sigmoid(x):
    return 0.5 * jnp.tanh(0.5 * x) + 0.5


def _mm(a, b):
    return jnp.dot(a.astype(BF16), b.astype(BF16), preferred_element_type=F32)


def _mm_nt(a, b):
    return lax.dot_general(a.astype(BF16), b.astype(BF16), _NT, preferred_element_type=F32)


def _split2(x):
    hi = x.astype(BF16)
    return hi, (x - hi.astype(F32)).astype(BF16)


def _head_sum(x, ones_bf16):
    return jnp.dot(x.astype(BF16), ones_bf16, preferred_element_type=F32)


def _mm_ones_left(ones_bf16, x):
    hi, lo = _split2(x)
    d = functools.partial(jnp.dot, preferred_element_type=F32)
    return d(ones_bf16, hi) + d(ones_bf16, lo)


def _rms(x):
    return x * lax.rsqrt(jnp.mean(x * x, -1, keepdims=True) + NORM_EPS)


def _at_layer(shape, layer):
    zeros = (0,) * len(shape)
    return pl.BlockSpec((None,) + tuple(shape), lambda *_: (layer,) + zeros)


def _mod_row(i, n_ctx_tiles, tiles_per_lat):
    return jnp.where(i < n_ctx_tiles, 0, 1 + (i - n_ctx_tiles) // tiles_per_lat)


def _mod_kernel(cond_ref, w_ref, b_ref, o_ref):
    c = cond_ref[...]
    s = c * _sigmoid(c)
    o_ref[0] = _mm(s, w_ref[0]) + b_ref[0]


def _modulation(cond, w_mod, b_mod):
    n_cond = cond.shape[0]
    n_col = 6 * D_MODEL // MOD_COLS
    return pl.pallas_call(
        _mod_kernel,
        grid=(DEPTH, n_col),
        in_specs=[pl.BlockSpec((n_cond, D_MODEL), lambda l, j: (0, 0)),
                  pl.BlockSpec((1, D_MODEL, MOD_COLS), lambda l, j: (l, 0, j)),
                  pl.BlockSpec((1, 1, MOD_COLS), lambda l, j: (l, 0, j))],
        out_specs=pl.BlockSpec((1, n_cond, MOD_COLS), lambda l, j: (l, 0, j)),
        out_shape=jax.ShapeDtypeStruct((DEPTH, n_cond, 6 * D_MODEL), F32),
        compiler_params=_params(2),
        name="modulation",
    )(cond, w_mod, b_mod.reshape(DEPTH, 1, 6 * D_MODEL))


def _inproj_kernel(x_ref, mod_ref, g_ref, w_ref, q_ref, kv_ref, rkv_ref, lora_ref):
    m = mod_ref[0]
    h = _rms(x_ref[...]) * g_ref[...]
    h = h * (1.0 + m[1:2]) + m[0:1]
    u = jnp.dot(h.astype(BF16), w_ref[...], preferred_element_type=F32)
    q_ref[...] = u[:, :ATT_Q]
    kv_ref[...] = u[:, ATT_Q:ATT_Q + 2 * ATT_KV]
    rkv_ref[...] = u[:, ATT_Q + 2 * ATT_KV:D_IN]
    lora_ref[...] = u[:, D_IN:]


def _inproj(x, mods, g, w_cat, layer, mod_map):
    nt = x.shape[0]
    row = lambda i: (i, 0)
    return pl.pallas_call(
        _inproj_kernel,
        grid=(nt // TILE,),
        in_specs=[pl.BlockSpec((TILE, D_MODEL), row),
                  pl.BlockSpec((None, 1, 6, D_MODEL), lambda i: (layer, mod_map(i), 0, 0)),
                  _at_layer((1, D_MODEL), layer),
                  _at_layer((D_MODEL, D_IN + LORA_ALL), layer)],
        out_specs=[pl.BlockSpec((TILE, ATT_Q), row),
                   pl.BlockSpec((TILE, 2 * ATT_KV), row),
                   pl.BlockSpec((TILE, 3 * RW), row),
                   pl.BlockSpec((TILE, LORA_ALL), row)],
        out_shape=[jax.ShapeDtypeStruct((nt, ATT_Q), F32),
                   jax.ShapeDtypeStruct((nt, 2 * ATT_KV), F32),
                   jax.ShapeDtypeStruct((nt, 3 * RW), F32),
                   jax.ShapeDtypeStruct((nt, LORA_ALL), F32)],
        compiler_params=_params(1),
        name="inproj",
    )(x, mods, g, w_cat)


def _prep_kernel(rkv_ref, prev_ref, next_ref, lora_ref, mu_ref, w0_ref, w2_ref, a0_ref, a2_ref,
                 g2_ref, kk_ref, ka_ref, rk_ref, bd_ref, tri_ref,
                 v_out, al_out, be_out, ka_out, rh_out, wl_out, bonus_out, g_out,
                 *, n_ctx_tiles, tiles_per_lat):
    i = pl.program_id(0)
    j = (i - n_ctx_tiles) % tiles_per_lat
    is_ctx = i < n_ctx_tiles
    first = is_ctx | (j == 0)
    last = is_ctx | (j == tiles_per_lat - 1)

    u = rkv_ref[...]
    row = lax.broadcasted_iota(jnp.int32, (TILE, 1), 0)
    p_row = prev_ref[7:8, :] * (1.0 - first.astype(F32))
    n_row = next_ref[0:1, :] * (1.0 - last.astype(F32))
    prev = jnp.where(row == 0, p_row, pltpu.roll(u, 1, 0))
    nxt = jnp.where(row == TILE - 1, n_row, pltpu.roll(u, TILE - 1, 0))
    us = u + mu_ref[...] * (0.5 * (prev + nxt) - u)
    r = us[:, :RW]
    k = us[:, RW:2 * RW]
    v = us[:, 2 * RW:]

    bd = bd_ref[...]
    kk = k * kk_ref[...]
    kk = kk * lax.rsqrt(_head_sum(kk * kk, bd) + 1e-12)

    lora = lora_ref[...]
    lw_in = jnp.tanh(lora[:, :2 * LORA_W])
    la_in = lora[:, 2 * LORA_W:2 * LORA_W + 2 * LORA_A]
    bonus = jnp.zeros_like(v)
    for d in range(2):
        w_pre = w0_ref[d:d + 1, :] + _mm(lw_in, w2_ref[d])
        lw = -jnp.exp(-0.5) * _sigmoid(w_pre)
        a = _sigmoid(a0_ref[d:d + 1, :] + _mm(la_in, a2_ref[d]))
        kd = k * (1.0 + (a - 1.0) * ka_ref[...])
        bonus = bonus + _head_sum(r * kd * rk_ref[...], bd) * v
        cw = _mm_ones_left(tri_ref[d], lw)
        e_neg = jnp.exp(-cw)
        al_out[d] = (kk * jnp.exp(cw - lw)).astype(BF16)
        be_out[d] = (kk * a * e_neg).astype(BF16)
        ka_out[d] = (kd * e_neg).astype(BF16)
        rh_out[d] = (r * jnp.exp(cw)).astype(BF16)
        for c in range(TILE // CHUNK):
            end = c * CHUNK + (CHUNK - 1 if d == 0 else 0)
            wl_out[0, 4 * d + c:4 * d + c + 1, :] = jnp.exp(cw[end:end + 1, :])
    v_out[...] = v.astype(BF16)
    bonus_out[...] = bonus
    g_out[...] = _mm(_sigmoid(lora[:, 2 * LORA_W + 2 * LORA_A:]), g2_ref[...])


def _prep(rkv, lora, p, layer, consts, n_ctx_tiles, tiles_per_lat):
    nt = rkv.shape[0]
    n_tiles = nt // TILE
    row = lambda i: (i, 0)
    drow = lambda i: (0, i, 0)
    c2 = lambda i: (0, 0)
    c3 = lambda i: (0, 0, 0)
    hb = TILE // 8
    kern = functools.partial(_prep_kernel, n_ctx_tiles=n_ctx_tiles, tiles_per_lat=tiles_per_lat)
    dshape = jax.ShapeDtypeStruct((2, nt, RW), BF16)
    return pl.pallas_call(
        kern,
        grid=(n_tiles,),
        in_specs=[pl.BlockSpec((TILE, 3 * RW), row),
                  pl.BlockSpec((8, 3 * RW), lambda i: (jnp.maximum(i * hb - 1, 0), 0)),
                  pl.BlockSpec((8, 3 * RW), lambda i: (jnp.minimum((i + 1) * hb, nt // 8 - 1), 0)),
                  pl.BlockSpec((TILE, LORA_ALL), row),
                  _at_layer((1, 3 * RW), layer),
                  _at_layer((2, RW), layer),
                  _at_layer((2, 2 * LORA_W, RW), layer),
                  _at_layer((2, RW), layer),
                  _at_layer((2, 2 * LORA_A, RW), layer),
                  _at_layer((LORA_G, RW), layer),
                  _at_layer((1, RW), layer),
                  _at_layer((1, RW), layer),
                  _at_layer((1, RW), layer),
                  pl.BlockSpec((RW, RW), c2),
                  pl.BlockSpec((2, TILE, TILE), c3)],
        out_specs=[pl.BlockSpec((TILE, RW), row),
                   pl.BlockSpec((2, TILE, RW), drow),
                   pl.BlockSpec((2, TILE, RW), drow),
                   pl.BlockSpec((2, TILE, RW), drow),
                   pl.BlockSpec((2, TILE, RW), drow),
                   pl.BlockSpec((1, 8, RW), lambda i: (i, 0, 0)),
                   pl.BlockSpec((TILE, RW), row),
                   pl.BlockSpec((TILE, RW), row)],
        out_shape=[jax.ShapeDtypeStruct((nt, RW), BF16), dshape, dshape, dshape, dshape,
                   jax.ShapeDtypeStruct((n_tiles, 8, RW), F32),
                   jax.ShapeDtypeStruct((nt, RW), F32),
                   jax.ShapeDtypeStruct((nt, RW), F32)],
        compiler_params=_params(1),
        name="rwkv_prep",
    )(rkv, rkv, rkv, lora, p["mu"], p["w0"], p["w2"], p["a0"], p["a2"], p["g2"],
      p["k_k"], p["k_a"], p["r_k"], consts["bd"], consts["tri"])


def _scan_kernel(*refs, n_ctx_tiles, tiles_per_lat, half):
    n_in = 13
    ins = [refs[n_in * sl:n_in * (sl + 1)] for sl in range(2)]
    outs = [refs[2 * n_in + 3 * sl:2 * n_in + 3 * (sl + 1)] for sl in range(2)]
    s_scr = refs[-1]
    for sl in range(2):
        tile = sl * half + pl.program_id(0)
        is_ctx = tile < n_ctx_tiles
        seq_start = (tile - n_ctx_tiles) % tiles_per_lat == 0

        @pl.when(is_ctx)
        def _():
            s_scr[sl] = jnp.zeros(s_scr.shape[1:], F32)

        @pl.when(jnp.logical_not(is_ctx) & seq_start)
        def _():
            s_scr[sl] = ins[sl][12][...]

    lo = lax.broadcasted_iota(jnp.int32, (CHUNK, PAIR), 1) < R_HEAD
    rt = lax.broadcasted_iota(jnp.int32, (CHUNK, PAIR), 0)
    ct = lax.broadcasted_iota(jnp.int32, (CHUNK, PAIR), 1) % CHUNK
    before = [ct < rt, ct > rt]
    upto = [ct <= rt, ct >= rt]
    same_head = (lax.broadcasted_iota(jnp.int32, (PAIR, PAIR), 0) // R_HEAD
                 == lax.broadcasted_iota(jnp.int32, (PAIR, PAIR), 1) // R_HEAD)
    n_sq = CHUNK.bit_length() - 2
    n_chunks = TILE // CHUNK

    def diag(x):
        x = x.astype(BF16)
        zero = jnp.zeros_like(x)
        return jnp.concatenate([jnp.where(lo, x, zero), jnp.where(lo, zero, x)], 0)

    def chunk(c, carry):
        chains = []
        for sl in range(2):
            for d in range(2):
                al, be, ka, rh, v, wl = ins[sl][6 * d:6 * (d + 1)]
                y_ref = outs[sl][d]
                ce = c if d == 0 else n_chunks - 1 - c
                rows = pl.ds(pl.multiple_of(ce * CHUNK, CHUNK), CHUNK)
                wl_all = wl[pl.ds(n_chunks * d + ce, 1), :]
                for p in range(N_PAIRS):
                    lanes = slice(PAIR * p, PAIR * (p + 1))
                    chains.append(dict(
                        sl=sl, d=d, p=p, rows=rows, lanes=lanes, y_ref=y_ref,
                        wl=wl_all[:, lanes], a=al[rows, lanes], b=be[rows, lanes],
                        k=ka[rows, lanes], r=rh[rows, lanes], v=v[rows, lanes],
                        s=s_scr[sl, d, p]))

        zero = jnp.zeros((CHUNK, PAIR), F32)
        for ch in chains:
            ar = jnp.concatenate([ch["a"], ch["r"]], 0)
            ch["v_d"] = diag(ch["v"])
            bks = jnp.concatenate([diag(ch["b"]), diag(ch["k"]), ch["s"].astype(BF16)], 0)
            ch["gram"] = lax.dot_general(ar, bks, _NT, preferred_element_type=F32)
        for ch in chains:
            g, d = ch["gram"], ch["d"]
            ch["m_ab"] = jnp.where(before[d], g[:CHUNK, :PAIR], zero)
            ch["m_rb"] = jnp.where(upto[d], g[CHUNK:, :PAIR], zero)
            m_ak = jnp.where(before[d], g[:CHUNK, PAIR:2 * PAIR], zero)
            m_rk = jnp.where(upto[d], g[CHUNK:, PAIR:2 * PAIR], zero)
            ch["mv"] = _mm(jnp.concatenate([m_ak, m_rk], 0), ch["v_d"])

        for ch in chains:
            ch["x"] = ch["gram"][:CHUNK, 2 * PAIR:] + ch["mv"][:CHUNK]
            ch["pw"] = ch["m_ab"]
        for lvl in range(n_sq + 1):
            for ch in chains:
                if lvl < n_sq:
                    px = _mm(ch["pw"], jnp.concatenate([diag(ch["pw"]), diag(ch["x"])], 1))
                    ch["pw"] = px[:, :PAIR]
                    step = px[:, PAIR:]
                else:
                    step = _mm(ch["pw"], diag(ch["x"]))
                ch["x"] = ch["x"] - step if lvl == 0 else ch["x"] + step

        for ch in chains:
            z = -ch["x"]
            ch["y"] = ch["gram"][CHUNK:, 2 * PAIR:] + ch["mv"][CHUNK:] + _mm(ch["m_rb"], diag(z))
            ch["zv_t"] = jnp.concatenate([z, ch["v"].astype(F32)], 0).T
        for ch in chains:
            bk = jnp.concatenate([ch["b"], ch["k"]], 0)
            upd = jnp.where(same_head, _mm(ch["zv_t"], bk), 0.0)
            ch["s_new"] = (ch["s"] + upd) * ch["wl"]
        for ch in chains:
            ch["y_ref"][ch["rows"], ch["lanes"]] = ch["y"]
            s_scr[ch["sl"], ch["d"], ch["p"]] = ch["s_new"]
        return carry

    lax.fori_loop(0, n_chunks, chunk, 0)
    for sl in range(2):
        st_ref = outs[sl][2]
        for d in range(2):
            for p in range(N_PAIRS):
                s = s_scr[sl, d, p]
                st_ref[d, 2 * p] = s[:R_HEAD, :R_HEAD]
                st_ref[d, 2 * p + 1] = s[R_HEAD:, R_HEAD:]


def _expand_state(s0):
    n = s0.shape[0]
    s = s0.reshape(n, 2, N_PAIRS, 2, R_HEAD, R_HEAD)
    z = jnp.zeros_like(s[:, :, :, 0])
    top = jnp.concatenate([s[:, :, :, 0], z], -1)
    bot = jnp.concatenate([z, s[:, :, :, 1]], -1)
    return jnp.concatenate([top, bot], -2)


def _scan(al, be, ka, rh, v, wl, s0_lat, n_ctx_tiles, tiles_per_lat):
    nt = v.shape[0]
    n_tiles = nt // TILE
    half = n_tiles // 2
    n_lat = s0_lat.shape[0]
    assert n_tiles % 2 == 0 and (half <= n_ctx_tiles or (half - n_ctx_tiles) % tiles_per_lat == 0)

    def mirror(t):
        j = (t - n_ctx_tiles) % tiles_per_lat
        return jnp.where(t < n_ctx_tiles, t, t - j + (tiles_per_lat - 1 - j))

    lat_seq = lambda t: jnp.clip((t - n_ctx_tiles) // tiles_per_lat, 0, n_lat - 1)
    state_in = (None, 2, N_PAIRS, PAIR, PAIR)
    state_out = (None, 2, R_HEADS, R_HEAD, R_HEAD)
    in_specs, out_specs, out_shape, n_ctx_slot = [], [], [], []
    for sl in range(2):
        t0 = sl * half
        fwd3 = pl.BlockSpec((None, TILE, RW), lambda i, t0=t0: (0, t0 + i, 0))
        bwd3 = pl.BlockSpec((None, TILE, RW), lambda i, t0=t0: (1, mirror(t0 + i), 0))
        fwd2 = pl.BlockSpec((TILE, RW), lambda i, t0=t0: (t0 + i, 0))
        bwd2 = pl.BlockSpec((TILE, RW), lambda i, t0=t0: (mirror(t0 + i), 0))
        wl_f = pl.BlockSpec((None, 8, RW), lambda i, t0=t0: (t0 + i, 0, 0))
        wl_b = pl.BlockSpec((None, 8, RW), lambda i, t0=t0: (mirror(t0 + i), 0, 0))
        in_specs += [fwd3, fwd3, fwd3, fwd3, fwd2, wl_f, bwd3, bwd3, bwd3, bwd3, bwd2, wl_b,
                     pl.BlockSpec(state_in, lambda i, t0=t0: (lat_seq(t0 + i), 0, 0, 0, 0))]
        n_c = min(max(n_ctx_tiles - t0, 0), half)
        n_park = 1 if n_c < half else 0
        n_ctx_slot.append(n_c)
        out_specs += [pl.BlockSpec((TILE, RW), lambda i: (i, 0)),
                      pl.BlockSpec((TILE, RW), lambda i, t0=t0: (mirror(t0 + i) - t0, 0)),
                      pl.BlockSpec(state_out, lambda i, n_c=n_c: (jnp.minimum(i, n_c), 0, 0, 0, 0))]
        out_shape += [jax.ShapeDtypeStruct((half * TILE, RW), F32),
                      jax.ShapeDtypeStruct((half * TILE, RW), F32),
                      jax.ShapeDtypeStruct((n_c + n_park, 2, R_HEADS, R_HEAD, R_HEAD), F32)]
    kern = functools.partial(_scan_kernel, n_ctx_tiles=n_ctx_tiles, tiles_per_lat=tiles_per_lat,
                             half=half)
    s0 = _expand_state(s0_lat)
    args = (al, be, ka, rh, v, wl, al, be, ka, rh, v, wl, s0)
    ya0, ya1, st_a, yb0, yb1, st_b = pl.pallas_call(
        kern,
        grid=(half,),
        in_specs=in_specs,
        out_specs=out_specs,
        out_shape=out_shape,
        scratch_shapes=[pltpu.VMEM((2, 2, N_PAIRS, PAIR, PAIR), F32)],
        compiler_params=_params(0, 1),
        name="wkv_scan",
    )(*args, *args)
    st = jnp.concatenate([st_a[:n_ctx_slot[0]], st_b[:n_ctx_slot[1]]], 0)
    return (ya0, yb0), (ya1, yb1), st


def _dup_halves(x, lo):
    xr = pltpu.roll(x, HEAD_DIM, 1)
    return jnp.where(lo, x, xr), jnp.where(lo, xr, x)


def _stack_heads(qp, lo):
    zero = jnp.zeros_like(qp)
    return jnp.concatenate([jnp.where(lo, qp, zero), jnp.where(lo, zero, qp)], 0)


def _sink_row(sink_ref, p, rows):
    c = lax.broadcasted_iota(jnp.int32, (1, 2 * rows), 1)
    return jnp.where(c < rows, sink_ref[2 * p:2 * p + 1, 0:1], sink_ref[2 * p + 1:2 * p + 2, 0:1])


def _softmax_pv(s_t, sk, v_t):
    m = jnp.maximum(jnp.max(s_t, 0, keepdims=True), sk)
    e = jnp.exp(s_t - m)
    den = jnp.sum(e, 0, keepdims=True) + jnp.exp(sk - m)
    return jnp.dot(v_t, e.astype(BF16), preferred_element_type=F32) / den


def _unstack_heads_t(o_t, rows):
    top = lax.broadcasted_iota(jnp.int32, (PAIR, rows), 0) < HEAD_DIM
    return jnp.where(top, o_t[:, :rows], o_t[:, rows:]).T


def _attn_ctx_kernel(q_ref, kv_ref, sink_ref, o_ref):
    rows = q_ref.shape[0]
    lo = lax.broadcasted_iota(jnp.int32, (rows, PAIR), 1) < HEAD_DIM
    k_dup = [k.astype(BF16) for k in _dup_halves(kv_ref[:, :ATT_KV], lo)]
    v_t = [v.T.astype(BF16) for v in _dup_halves(kv_ref[:, ATT_KV:], lo)]
    scale = HEAD_DIM ** -0.5
    pairs = range(N_HEADS // 2)
    kv_head = [(2 * p) // (N_HEADS // KV_HEADS) for p in pairs]
    qs = [_stack_heads(q_ref[:, PAIR * p:PAIR * (p + 1)] * scale, lo).astype(BF16) for p in pairs]
    s_t = [lax.dot_general(k_dup[kv_head[p]], qs[p], _NT, preferred_element_type=F32)
           for p in pairs]
    o_t = [_softmax_pv(s_t[p], _sink_row(sink_ref, p, rows), v_t[kv_head[p]]) for p in pairs]
    for p in pairs:
        o_ref[:, PAIR * p:PAIR * (p + 1)] = _unstack_heads_t(o_t[p], rows)


def _attn_ctx(q, kv, sink_rows, layer, n_seq, seq_len):
    return pl.pallas_call(
        _attn_ctx_kernel,
        grid=(n_seq,),
        in_specs=[pl.BlockSpec((seq_len, ATT_Q), lambda b: (b, 0)),
                  pl.BlockSpec((seq_len, 2 * ATT_KV), lambda b: (b, 0)),
                  _at_layer((N_HEADS, 128), layer)],
        out_specs=pl.BlockSpec((seq_len, ATT_Q), lambda b: (b, 0)),
        out_shape=jax.ShapeDtypeStruct((n_seq * seq_len, ATT_Q), F32),
        compiler_params=_params(1),
        name="attn_ctx",
    )(q, kv, sink_rows)


def _rope(x, cos, sgn_sin, first16):
    sw = jnp.where(first16, pltpu.roll(x, PAIR - 16, 1), pltpu.roll(x, 16, 1))
    return x * cos + sw * sgn_sin


def _attn_lat_kernel(q_ref, kv_ref, ck_ref, cv_ref, cos_ref, sin_ref, sink_ref, o_ref,
                     kd_scr, vt_scr, ckd_scr, cvt_scr, *, seq_len):
    i = pl.program_id(1)
    qb = q_ref.shape[0]
    n_blk = seq_len // qb

    @pl.when(i == 0)
    def _():
        lo_s = lax.broadcasted_iota(jnp.int32, (seq_len, PAIR), 1) < HEAD_DIM
        f16_s = (lax.broadcasted_iota(jnp.int32, (seq_len, PAIR), 1) % 32) < 16
        k_r = _rope(kv_ref[:, :ATT_KV], cos_ref[...], sin_ref[...], f16_s)
        k_dup = _dup_halves(k_r, lo_s)
        v_dup = _dup_halves(kv_ref[:, ATT_KV:], lo_s)
        lo_c = lax.broadcasted_iota(jnp.int32, ck_ref.shape[1:], 1) < HEAD_DIM
        ck_dup = _dup_halves(ck_ref[0], lo_c)
        cv_dup = _dup_halves(cv_ref[0], lo_c)
        for g in range(KV_HEADS):
            for b in range(n_blk):
                kd_scr[g, b] = k_dup[g][b * qb:(b + 1) * qb].astype(BF16)
                vt_scr[g, b] = v_dup[g][b * qb:(b + 1) * qb].T.astype(BF16)
            ckd_scr[g] = ck_dup[g].astype(BF16)
            cvt_scr[g] = cv_dup[g].T.astype(BF16)

    lo = lax.broadcasted_iota(jnp.int32, (qb, PAIR), 1) < HEAD_DIM
    f16 = (lax.broadcasted_iota(jnp.int32, (qb, PAIR), 1) % 32) < 16
    q0 = pl.multiple_of(i * qb, qb)
    cos = cos_ref[pl.ds(q0, qb), :]
    sin = sin_ref[pl.ds(q0, qb), :]
    qpos = i * qb + lax.broadcasted_iota(jnp.int32, (qb, 2 * qb), 1) % qb
    krow = lax.broadcasted_iota(jnp.int32, (qb, 2 * qb), 0)
    scale = HEAD_DIM ** -0.5

    pairs = range(N_HEADS // 2)
    kv_head = [(2 * p) // (N_HEADS // KV_HEADS) for p in pairs]
    win = []
    for jj in range(3):
        j = i - 1 + jj
        jc = jnp.clip(j, 0, n_blk - 1)
        kpos = jnp.where(j == jc, jc * qb, -4 * seq_len) + krow
        win.append((jc, jnp.abs(qpos - kpos) <= WINDOW))

    qs = [_stack_heads(_rope(q_ref[:, PAIR * p:PAIR * (p + 1)], cos, sin, f16) * scale,
                       lo).astype(BF16) for p in pairs]
    s_t = []
    for p in pairs:
        g = kv_head[p]
        nt = functools.partial(lax.dot_general, dimension_numbers=_NT, preferred_element_type=F32)
        parts = [nt(ckd_scr[g], qs[p])]
        for jc, ok in win:
            parts.append(jnp.where(ok, nt(kd_scr[g, jc], qs[p]), NEG_INF))
        s_t.append(jnp.concatenate(parts, 0))
    o_t = []
    for p in pairs:
        g = kv_head[p]
        v_t = jnp.concatenate([cvt_scr[g]] + [vt_scr[g, jc] for jc, _ in win], 1)
        o_t.append(_softmax_pv(s_t[p], _sink_row(sink_ref, p, qb), v_t))
    for p in pairs:
        o_ref[:, PAIR * p:PAIR * (p + 1)] = _unstack_heads_t(o_t[p], qb)


def _attn_lat(q, kv, ck, cv, layer, cos_t, sin_t, sink_rows, row0, n_seq, seq_len):
    qb = WINDOW
    n_blk = seq_len // qb
    blk0 = row0 // qb
    seq0 = row0 // seq_len
    n_past = ck.shape[2]
    kern = functools.partial(_attn_lat_kernel, seq_len=seq_len)
    return pl.pallas_call(
        kern,
        grid=(n_seq, n_blk),
        in_specs=[pl.BlockSpec((qb, ATT_Q), lambda b, i: (blk0 + b * n_blk + i, 0)),
                  pl.BlockSpec((seq_len, 2 * ATT_KV), lambda b, i: (seq0 + b, 0)),
                  pl.BlockSpec((1, None, n_past, ATT_KV), lambda b, i: (b, layer, 0, 0)),
                  pl.BlockSpec((1, None, n_past, ATT_KV), lambda b, i: (b, layer, 0, 0)),
                  pl.BlockSpec((seq_len, PAIR), lambda b, i: (0, 0)),
                  pl.BlockSpec((seq_len, PAIR), lambda b, i: (0, 0)),
                  _at_layer((N_HEADS, 128), layer)],
        out_specs=pl.BlockSpec((qb, ATT_Q), lambda b, i: (b * n_blk + i, 0)),
        out_shape=jax.ShapeDtypeStruct((n_seq * seq_len, ATT_Q), F32),
        scratch_shapes=[pltpu.VMEM((KV_HEADS, n_blk, qb, PAIR), BF16),
                        pltpu.VMEM((KV_HEADS, n_blk, PAIR, qb), BF16),
                        pltpu.VMEM((KV_HEADS, n_past, PAIR), BF16),
                        pltpu.VMEM((KV_HEADS, PAIR, n_past), BF16)],
        compiler_params=_params(1, 1),
        name="attn_lat",
    )(q, kv, ck, cv, cos_t, sin_t, sink_rows)


def _postmix_kernel(x_ref, octx_ref, olat_ref, ya0_ref, yb0_ref, ya1_ref, yb1_ref, bonus_ref,
                    g_ref, mod_ref,
                    gpost_ref, gpre_ref, lng_ref, lnb_ref, wo_ref, bd_ref, router_ref,
                    x1_ref, h_ref, gates_ref, *, with_router, n_ctx_tiles, half):
    m = mod_ref[0]
    bd = bd_ref[...]
    tile = lax.broadcasted_iota(jnp.int32, octx_ref.shape, 0) * 0 + pl.program_id(0)
    o_att = jnp.where(tile < n_ctx_tiles, octx_ref[...], olat_ref[...])
    in_a = (lax.broadcasted_iota(jnp.int32, ya0_ref.shape, 0) * 0 + pl.program_id(0)) < half
    y = jnp.where(in_a, ya0_ref[...] + ya1_ref[...], yb0_ref[...] + yb1_ref[...])
    inv_n = 1.0 / R_HEAD
    mu = _head_sum(y, bd) * inv_n
    yc = y - mu
    var = _head_sum(yc * yc, bd) * inv_n
    yn = yc * lax.rsqrt(var + GN_EPS)
    o_rw = (yn * lng_ref[...] + lnb_ref[...] + bonus_ref[...]) * g_ref[...]
    o = (jnp.dot(o_att.astype(BF16), wo_ref[:ATT_Q, :], preferred_element_type=F32)
         + jnp.dot(o_rw.astype(BF16), wo_ref[ATT_Q:, :], preferred_element_type=F32))
    x1 = x_ref[...] + m[2:3] * (_rms(o) * gpost_ref[...])
    x1_ref[...] = x1
    h = (_rms(x1) * gpre_ref[...]) * (1.0 + m[4:5]) + m[3:4]
    h_ref[...] = h.astype(BF16)
    if with_router:
        h_hi, h_lo = _split2(h)
        r_hi, r_lo = _split2(router_ref[...])
        d = functools.partial(jnp.dot, preferred_element_type=F32)
        logits = d(h_hi, r_hi) + (d(h_hi, r_lo) + d(h_lo, r_hi))
        lane = lax.broadcasted_iota(jnp.int32, logits.shape, 1)
        valid = lane < N_EXPERTS
        logits = jnp.where(valid, logits, NEG_INF)
        e = jnp.exp(logits - jnp.max(logits, -1, keepdims=True))
        probs = e / jnp.sum(e, -1, keepdims=True)
        lane_f = lane.astype(F32)
        p1 = jnp.max(probs, -1, keepdims=True)
        i1 = jnp.min(jnp.where(probs == p1, lane_f, 1e9), -1, keepdims=True)
        rest = jnp.where(lane_f == i1, -1.0, probs)
        p2 = jnp.max(rest, -1, keepdims=True)
        i2 = jnp.min(jnp.where(rest == p2, lane_f, 1e9), -1, keepdims=True)
        sel = (lane_f == i1) | (lane_f == i2)
        gates_ref[...] = jnp.where(sel, probs / (p1 + p2), 0.0)
    else:
        gates_ref[...] = jnp.ones(gates_ref.shape, F32)


def _postmix(x, o_ctx, o_lat, y0, y1, bonus, g, mods, p, w_o, layer, consts, mod_map,
             router_pad, router_layer, n_ctx_tiles):
    nt = x.shape[0]
    row = lambda i: (i, 0)
    c2 = lambda i: (0, 0)
    with_router = router_layer is not None
    half = nt // TILE // 2
    kern = functools.partial(_postmix_kernel, with_router=with_router, n_ctx_tiles=n_ctx_tiles,
                             half=half)
    n_lat_tiles = nt // TILE - n_ctx_tiles
    first_half = pl.BlockSpec((TILE, RW), lambda i: (jnp.minimum(i, half - 1), 0))
    second_half = pl.BlockSpec((TILE, RW), lambda i: (jnp.maximum(i - half, 0), 0))
    return pl.pallas_call(
        kern,
        grid=(nt // TILE,),
        in_specs=[pl.BlockSpec((TILE, D_MODEL), row),
                  pl.BlockSpec((TILE, ATT_Q), lambda i: (jnp.minimum(i, n_ctx_tiles - 1), 0)),
                  pl.BlockSpec((TILE, ATT_Q),
                               lambda i: (jnp.clip(i - n_ctx_tiles, 0, n_lat_tiles - 1), 0)),
                  first_half, second_half, first_half, second_half,
                  pl.BlockSpec((TILE, RW), row),
                  pl.BlockSpec((TILE, RW), row),
                  pl.BlockSpec((None, 1, 6, D_MODEL), lambda i: (layer, mod_map(i), 0, 0)),
                  _at_layer((1, D_MODEL), layer),
                  _at_layer((1, D_MODEL), layer),
                  _at_layer((1, RW), layer),
                  _at_layer((1, RW), layer),
                  _at_layer((ATT_Q + RW, D_MODEL), layer),
                  pl.BlockSpec((RW, RW), c2),
                  _at_layer((D_MODEL, 128), router_layer or 0)],
        out_specs=[pl.BlockSpec((TILE, D_MODEL), row),
                   pl.BlockSpec((TILE, D_MODEL), row),
                   pl.BlockSpec((TILE, 128), row)],
        out_shape=[jax.ShapeDtypeStruct((nt, D_MODEL), F32),
                   jax.ShapeDtypeStruct((nt, D_MODEL), BF16),
                   jax.ShapeDtypeStruct((nt, 128), F32)],
        compiler_params=_params(1),
        name="postmix",
    )(x, o_ctx, o_lat, y0[0], y0[1], y1[0], y1[1], bonus, g, mods, p["g_post_mix"],
      p["g_pre_ffn"],
      p["ln_g"], p["ln_b"], w_o, consts["bd"], router_pad)


def _ffn_kernel(h_ref, x1_ref, gates_ref, mod_ref, gpost_ref, w1_ref, w3_ref, w2_ref, o_ref,
                acc_ref):
    e = pl.program_id(1)
    f = pl.program_id(2)
    first = (e == 0) & (f == 0)
    last = (e == pl.num_programs(1) - 1) & (f == pl.num_programs(2) - 1)

    @pl.when(first)
    def _():
        acc_ref[...] = jnp.zeros_like(acc_ref)

    h = h_ref[...]
    lane = lax.broadcasted_iota(jnp.int32, gates_ref.shape, 1)
    gate = jnp.sum(jnp.where(lane == e, gates_ref[...], 0.0), -1, keepdims=True)
    a = jnp.dot(h, w1_ref[0], preferred_element_type=F32)
    b = jnp.dot(h, w3_ref[0], preferred_element_type=F32)
    t = (a * _sigmoid(a)) * b
    acc_ref[...] += gate * jnp.dot(t.astype(BF16), w2_ref[0], preferred_element_type=F32)

    @pl.when(last)
    def _():
        m = mod_ref[0]
        o_ref[...] = x1_ref[...] + m[5:6] * (_rms(acc_ref[...]) * gpost_ref[...])


def _ffn(h, x1, gates, mods, g_post, w1, w3, w2, layer, w_layer, mod_map_ffn):
    nt = h.shape[0]
    _, n_e, _, d_ff = w1.shape
    n_f = d_ff // FF_BLOCK
    row = lambda i, e, f: (i, 0)
    return pl.pallas_call(
        _ffn_kernel,
        grid=(nt // FFN_TILE, n_e, n_f),
        in_specs=[pl.BlockSpec((FFN_TILE, D_MODEL), row),
                  pl.BlockSpec((FFN_TILE, D_MODEL), row),
                  pl.BlockSpec((FFN_TILE, 128), row),
                  pl.BlockSpec((None, 1, 6, D_MODEL),
                               lambda i, e, f: (layer, mod_map_ffn(i), 0, 0)),
                  _at_layer((1, D_MODEL), layer),
                  pl.BlockSpec((None, 1, D_MODEL, FF_BLOCK), lambda i, e, f: (w_layer, e, 0, f)),
                  pl.BlockSpec((None, 1, D_MODEL, FF_BLOCK), lambda i, e, f: (w_layer, e, 0, f)),
                  pl.BlockSpec((None, 1, FF_BLOCK, D_MODEL), lambda i, e, f: (w_layer, e, f, 0))],
        out_specs=pl.BlockSpec((FFN_TILE, D_MODEL), row),
        out_shape=jax.ShapeDtypeStruct((nt, D_MODEL), F32),
        scratch_shapes=[pltpu.VMEM((FFN_TILE, D_MODEL), F32)],
        compiler_params=_params(1, 2),
        name="ffn",
    )(h, x1, gates, mods, g_post, w1, w3, w2)


def _rope_tables(seq_len):
    half = HEAD_DIM // 2
    nf = half // 2
    inv = ROPE_THETA ** (-jnp.arange(nf, dtype=F32) / nf)
    t = jnp.arange(seq_len)
    row = (t // GRID_W).astype(F32)
    col = (t % GRID_W).astype(F32)
    lane = jnp.arange(PAIR)
    pos = jnp.where(((lane % HEAD_DIM) // half == 0)[None, :], row[:, None], col[:, None])
    ang = pos * inv[lane % nf][None, :]
    first = ((lane % half) < nf)[None, :]
    return jnp.cos(ang), jnp.where(first, -jnp.sin(ang), jnp.sin(ang))


def _constants():
    r = jnp.arange(RW)
    bd = (r[:, None] // R_HEAD == r[None, :] // R_HEAD).astype(BF16)
    t = jnp.arange(TILE)
    same = t[:, None] // CHUNK == t[None, :] // CHUNK
    tri = jnp.stack([same & (t[None, :] <= t[:, None]), same & (t[None, :] >= t[:, None])])
    return {"bd": bd, "tri": tri.astype(BF16)}


def _pad_rows(w2):
    z = jnp.zeros_like(w2[:, 0])
    return jnp.stack([jnp.concatenate([w2[:, 0], z], 1), jnp.concatenate([z, w2[:, 1]], 1)],
                     1).astype(BF16)


def kernel(x_prompt, x_sample, cache_k, cache_v, state_rwkv, c, c_ctx, w_mod, b_mod, g_pre_mix, g_post_mix, g_pre_ffn, g_post_ffn, w_in, mu_shift, w_o, attn_sink, rw_w0, rw_w1, rw_w2, rw_a0, rw_a1, rw_a2, rw_g1, rw_g2, rw_k_k, rw_k_a, rw_r_k, rw_ln_g, rw_ln_b, ffn_w1, ffn_w3, ffn_w2, moe_router, moe_w1, moe_w3, moe_w2):
    n_ctx, s_ctx, _ = x_prompt.shape
    n_lat, s_lat, _ = x_sample.shape
    n_past = cache_k.shape[2]
    assert s_ctx == TILE and s_lat % TILE == 0 and (n_ctx * s_ctx) % s_lat == 0
    nt_ctx = n_ctx * s_ctx
    n_ctx_tiles = nt_ctx // TILE
    tiles_per_lat = s_lat // TILE
    mod_map = functools.partial(_mod_row, n_ctx_tiles=n_ctx_tiles, tiles_per_lat=tiles_per_lat)
    ffn_per = FFN_TILE // TILE
    mod_map_ffn = lambda i: mod_map(i * ffn_per)

    x = jnp.concatenate([x_prompt.reshape(nt_ctx, D_MODEL), x_sample.reshape(n_lat * s_lat, D_MODEL)], 0)
    n_cond = 8
    cond = jnp.concatenate([c_ctx[None, :], c, jnp.zeros((n_cond - 1 - n_lat, D_MODEL), F32)], 0)
    mods = _modulation(cond, w_mod, b_mod).reshape(DEPTH, n_cond, 6, D_MODEL)

    consts = _constants()
    cos_t, sin_t = _rope_tables(s_lat)
    w_cat = jnp.concatenate(
        [w_in, rw_w1[:, 0], rw_w1[:, 1], rw_a1[:, 0], rw_a1[:, 1], rw_g1], 2).astype(BF16)
    w_o_b = w_o.astype(BF16)
    dense_w = [w.astype(BF16)[:, None] for w in (ffn_w1, ffn_w3, ffn_w2)]
    moe_w = [w.astype(BF16) for w in (moe_w1, moe_w3, moe_w2)]
    router_pad = jnp.pad(moe_router, ((0, 0), (0, 0), (0, 128 - N_EXPERTS)))
    sink_rows = jnp.broadcast_to(attn_sink[:, :, None], (DEPTH, N_HEADS, 128))
    ck = cache_k.reshape(n_lat, DEPTH, n_past, ATT_KV)
    cv = cache_v.reshape(n_lat, DEPTH, n_past, ATT_KV)

    vecs = lambda a: a.reshape(DEPTH, 1, -1)
    p = {"mu": vecs(mu_shift), "w0": rw_w0, "a0": rw_a0, "w2": _pad_rows(rw_w2),
         "a2": _pad_rows(rw_a2), "g2": rw_g2.astype(BF16), "k_k": vecs(rw_k_k),
         "k_a": vecs(rw_k_a), "r_k": vecs(rw_r_k), "ln_g": vecs(rw_ln_g), "ln_b": vecs(rw_ln_b),
         "g_post_mix": vecs(g_post_mix), "g_pre_ffn": vecs(g_pre_ffn)}
    g_pre_mix_v = vecs(g_pre_mix)
    g_post_ffn_v = vecs(g_post_ffn)

    new_k, new_v, new_s = [], [], []
    for l in range(DEPTH):
        q, kv, rkv, lora = _inproj(x, mods, g_pre_mix_v, w_cat, l, mod_map)
        v_b, al, be, ka, rh, wl, bonus, g = _prep(rkv, lora, p, l, consts, n_ctx_tiles,
                                                  tiles_per_lat)
        y0, y1, s_ctx_new = _scan(al, be, ka, rh, v_b, wl, state_rwkv[:, l],
                                  n_ctx_tiles, tiles_per_lat)
        o_ctx = _attn_ctx(q, kv, sink_rows, l, n_ctx, s_ctx)
        o_lat = _attn_lat(q, kv, ck, cv, l, cos_t, sin_t, sink_rows, nt_ctx, n_lat, s_lat)

        i = l // 2
        moe = l % 2 == 1
        x1, h, gates = _postmix(x, o_ctx, o_lat, y0, y1, bonus, g, mods, p, w_o_b, l, consts,
                                mod_map, router_pad, i if moe else None, n_ctx_tiles)
        w1, w3, w2 = moe_w if moe else dense_w
        x = _ffn(h, x1, gates, mods, g_post_ffn_v, w1, w3, w2, l, i, mod_map_ffn)

        new_k.append(kv[:nt_ctx, :ATT_KV].reshape(n_ctx, s_ctx, KV_HEADS, HEAD_DIM))
        new_v.append(kv[:nt_ctx, ATT_KV:].reshape(n_ctx, s_ctx, KV_HEADS, HEAD_DIM))
        new_s.append(s_ctx_new)

    y_prompt = x[:nt_ctx].reshape(n_ctx, s_ctx, D_MODEL)
    y_sample = x[nt_ctx:].reshape(n_lat, s_lat, D_MODEL)
    return (y_prompt, y_sample, jnp.stack(new_k, 1), jnp.stack(new_v, 1), jnp.stack(new_s, 1))
```

```python
import functools

import jax
import jax.numpy as jnp
from jax import lax
from jax.experimental import pallas as pl
from jax.experimental.pallas import tpu as pltpu

F32 = jnp.float32
BF16 = jnp.bfloat16

D_MODEL = 1024
DEPTH = 4
GRID_W = 64
N_HEADS = 8
KV_HEADS = 2
HEAD_DIM = 64
WINDOW = 128
ROPE_THETA = 10000.0
R_HEADS = 8
R_HEAD = 64
RW = R_HEADS * R_HEAD
LORA_W = 64
LORA_A = 64
LORA_G = 128
LORA_ALL = 2 * LORA_W + 2 * LORA_A + LORA_G
ATT_Q = N_HEADS * HEAD_DIM
ATT_KV = KV_HEADS * HEAD_DIM
D_IN = ATT_Q + 2 * ATT_KV + 3 * RW
D_FF = 2816
N_EXPERTS = 8
D_FF_EXPERT = 1408
NORM_EPS = 1e-6
GN_EPS = 64e-5
NEG_INF = -1e30

TILE = 256
CHUNK = 64
PAIR = 2 * R_HEAD
N_PAIRS = R_HEADS // 2
FFN_TILE = 512
FF_BLOCK = 1408
MOD_COLS = 1536
VMEM_LIMIT = 48 * 1024 * 1024

_NT = (((1,), (1,)), ((), ()))


def _params(n_parallel, n_arbitrary=0, vmem_limit=VMEM_LIMIT):
    sem = ("parallel",) * n_parallel + ("arbitrary",) * n_arbitrary
    return pltpu.CompilerParams(dimension_semantics=sem, vmem_limit_bytes=vmem_limit)


def _sigmoid(x):
    return 0.5 * jnp.tanh(0.5 * x) + 0.5


def _mm(a, b):
    return jnp.dot(a.astype(BF16), b.astype(BF16), preferred_element_type=F32)


def _mm_nt(a, b):
    return lax.dot_general(a.astype(BF16), b.astype(BF16), _NT, preferred_element_type=F32)


def _split2(x):
    hi = x.astype(BF16)
    return hi, (x - hi.astype(F32)).astype(BF16)


def _head_sum(x, ones_bf16):
    return jnp.dot(x.astype(BF16), ones_bf16, preferred_element_type=F32)


def _mm_ones_left(ones_bf16, x):
    hi, lo = _split2(x)
    d = functools.partial(jnp.dot, preferred_element_type=F32)
    return d(ones_bf16, hi) + d(ones_bf16, lo)


def _rms(x):
    return x * lax.rsqrt(jnp.mean(x * x, -1, keepdims=True) + NORM_EPS)


def _at_layer(shape, layer):
    zeros = (0,) * len(shape)
    return pl.BlockSpec((None,) + tuple(shape), lambda *_: (layer,) + zeros)


def _mod_row(i, n_ctx_tiles, tiles_per_lat):
    return jnp.where(i < n_ctx_tiles, 0, 1 + (i - n_ctx_tiles) // tiles_per_lat)


def _mod_kernel(cond_ref, w_ref, b_ref, o_ref):
    c = cond_ref[...]
    s = c * _sigmoid(c)
    o_ref[0] = _mm(s, w_ref[0]) + b_ref[0]


def _modulation(cond, w_mod, b_mod):
    n_cond = cond.shape[0]
    n_col = 6 * D_MODEL // MOD_COLS
    return pl.pallas_call(
        _mod_kernel,
        grid=(DEPTH, n_col),
        in_specs=[pl.BlockSpec((n_cond, D_MODEL), lambda l, j: (0, 0)),
                  pl.BlockSpec((1, D_MODEL, MOD_COLS), lambda l, j: (l, 0, j)),
                  pl.BlockSpec((1, 1, MOD_COLS), lambda l, j: (l, 0, j))],
        out_specs=pl.BlockSpec((1, n_cond, MOD_COLS), lambda l, j: (l, 0, j)),
        out_shape=jax.ShapeDtypeStruct((DEPTH, n_cond, 6 * D_MODEL), F32),
        compiler_params=_params(2),
        name="modulation",
    )(cond, w_mod, b_mod.reshape(DEPTH, 1, 6 * D_MODEL))


HALO = 8


def _inprep_kernel(x_ref, xp_ref, xn_ref, mod_ref, g_ref, w_ref, mu_ref, w0_ref, w2_ref, a0_ref,
                   a2_ref, g2_ref, kk_ref, ka_ref, rk_ref, bd_ref, tri_ref,
                   q_out, kv_out, v_out, al_out, be_out, ka_out, rh_out, wl_out, bonus_out, g_out,
                   *, n_ctx_tiles, tiles_per_lat):
    i = pl.program_id(0)
    j = (i - n_ctx_tiles) % tiles_per_lat
    is_ctx = i < n_ctx_tiles
    first = is_ctx | (j == 0)
    last = is_ctx | (j == tiles_per_lat - 1)

    m = mod_ref[0]
    x_all = jnp.concatenate([xp_ref[...], x_ref[...], xn_ref[...]], 0)
    h = (_rms(x_all) * g_ref[...]) * (1.0 + m[1:2]) + m[0:1]
    main = slice(HALO, HALO + TILE)
    h_main = h[main].astype(BF16)
    n_att = ATT_Q + 2 * ATT_KV
    u_all = jnp.dot(h.astype(BF16), w_ref[:, n_att:], preferred_element_type=F32)
    rkv = slice(0, D_IN - n_att)
    u = u_all[main, rkv]
    lora = u_all[main, D_IN - n_att:]

    row = lax.broadcasted_iota(jnp.int32, (TILE, 1), 0)
    p_row = u_all[HALO - 1:HALO, rkv] * (1.0 - first.astype(F32))
    n_row = u_all[HALO + TILE:HALO + TILE + 1, rkv] * (1.0 - last.astype(F32))
    prev = jnp.where(row == 0, p_row, pltpu.roll(u, 1, 0))
    nxt = jnp.where(row == TILE - 1, n_row, pltpu.roll(u, TILE - 1, 0))
    us = u + mu_ref[...] * (0.5 * (prev + nxt) - u)
    r = us[:, :RW]
    k = us[:, RW:2 * RW]
    v = us[:, 2 * RW:]

    bd = bd_ref[...]
    kk = k * kk_ref[...]
    kk = kk * lax.rsqrt(_head_sum(kk * kk, bd) + 1e-12)

    lw_in = jnp.tanh(lora[:, :2 * LORA_W])
    la_in = lora[:, 2 * LORA_W:2 * LORA_W + 2 * LORA_A]
    bonus = jnp.zeros_like(v)
    for d in range(2):
        w_pre = w0_ref[d:d + 1, :] + _mm(lw_in, w2_ref[d])
        lw = -jnp.exp(-0.5) * _sigmoid(w_pre)
        a = _sigmoid(a0_ref[d:d + 1, :] + _mm(la_in, a2_ref[d]))
        kd = k * (1.0 + (a - 1.0) * ka_ref[...])
        bonus = bonus + _head_sum(r * kd * rk_ref[...], bd) * v
        cw = _mm_ones_left(tri_ref[d], lw)
        e_neg = jnp.exp(-cw)
        al_out[d] = (kk * jnp.exp(cw - lw)).astype(BF16)
        be_out[d] = (kk * a * e_neg).astype(BF16)
        ka_out[d] = (kd * e_neg).astype(BF16)
        rh_out[d] = (r * jnp.exp(cw)).astype(BF16)
        for c in range(TILE // CHUNK):
            end = c * CHUNK + (CHUNK - 1 if d == 0 else 0)
            wl_out[0, 4 * d + c:4 * d + c + 1, :] = jnp.exp(cw[end:end + 1, :])
    v_out[...] = v.astype(BF16)
    bonus_out[...] = bonus
    g_out[...] = _mm(_sigmoid(lora[:, 2 * LORA_W + 2 * LORA_A:]), g2_ref[...])
    u_att = jnp.dot(h_main, w_ref[:, :n_att], preferred_element_type=F32)
    q_out[...] = u_att[:, :ATT_Q]
    kv_out[...] = u_att[:, ATT_Q:]


def _inprep(x, mods, g, w_cat, p, layer, consts, mod_map, n_ctx_tiles, tiles_per_lat):
    nt = x.shape[0]
    n_tiles = nt // TILE
    row = lambda i: (i, 0)
    drow = lambda i: (0, i, 0)
    hb = TILE // HALO
    kern = functools.partial(_inprep_kernel, n_ctx_tiles=n_ctx_tiles, tiles_per_lat=tiles_per_lat)
    dshape = jax.ShapeDtypeStruct((2, nt, RW), BF16)
    return pl.pallas_call(
        kern,
        grid=(n_tiles,),
        in_specs=[pl.BlockSpec((TILE, D_MODEL), row),
                  pl.BlockSpec((HALO, D_MODEL), lambda i: (jnp.maximum(i * hb - 1, 0), 0)),
                  pl.BlockSpec((HALO, D_MODEL),
                               lambda i: (jnp.minimum((i + 1) * hb, nt // HALO - 1), 0)),
                  pl.BlockSpec((None, 1, 6, D_MODEL), lambda i: (layer, mod_map(i), 0, 0)),
                  _at_layer((1, D_MODEL), layer),
                  _at_layer((D_MODEL, D_IN + LORA_ALL), layer),
                  _at_layer((1, 3 * RW), layer),
                  _at_layer((2, RW), layer),
                  _at_layer((2, 2 * LORA_W, RW), layer),
                  _at_layer((2, RW), layer),
                  _at_layer((2, 2 * LORA_A, RW), layer),
                  _at_layer((LORA_G, RW), layer),
                  _at_layer((1, RW), layer),
                  _at_layer((1, RW), layer),
                  _at_layer((1, RW), layer),
                  pl.BlockSpec((RW, RW), lambda i: (0, 0)),
                  pl.BlockSpec((2, TILE, TILE), lambda i: (0, 0, 0))],
        out_specs=[pl.BlockSpec((TILE, ATT_Q), row),
                   pl.BlockSpec((TILE, 2 * ATT_KV), row),
                   pl.BlockSpec((TILE, RW), row),
                   pl.BlockSpec((2, TILE, RW), drow),
                   pl.BlockSpec((2, TILE, RW), drow),
                   pl.BlockSpec((2, TILE, RW), drow),
                   pl.BlockSpec((2, TILE, RW), drow),
                   pl.BlockSpec((1, 8, RW), lambda i: (i, 0, 0)),
                   pl.BlockSpec((TILE, RW), row),
                   pl.BlockSpec((TILE, RW), row)],
        out_shape=[jax.ShapeDtypeStruct((nt, ATT_Q), F32),
                   jax.ShapeDtypeStruct((nt, 2 * ATT_KV), F32),
                   jax.ShapeDtypeStruct((nt, RW), BF16), dshape, dshape, dshape, dshape,
                   jax.ShapeDtypeStruct((n_tiles, 8, RW), F32),
                   jax.ShapeDtypeStruct((nt, RW), F32),
                   jax.ShapeDtypeStruct((nt, RW), F32)],
        compiler_params=_params(1),
        name="inproj_prep",
    )(x, x, x, mods, g, w_cat, p["mu"], p["w0"], p["w2"], p["a0"], p["a2"], p["g2"],
      p["k_k"], p["k_a"], p["r_k"], consts["bd"], consts["tri"])


def _scan_kernel(*refs, n_ctx_tiles, tiles_per_lat, half):
    n_in = 13
    ins = [refs[n_in * sl:n_in * (sl + 1)] for sl in range(2)]
    outs = [refs[2 * n_in + 3 * sl:2 * n_in + 3 * (sl + 1)] for sl in range(2)]
    s_scr = refs[-1]
    for sl in range(2):
        tile = sl * half + pl.program_id(0)
        is_ctx = tile < n_ctx_tiles
        seq_start = (tile - n_ctx_tiles) % tiles_per_lat == 0

        @pl.when(is_ctx)
        def _():
            s_scr[sl] = jnp.zeros(s_scr.shape[1:], F32)

        @pl.when(jnp.logical_not(is_ctx) & seq_start)
        def _():
            s_scr[sl] = ins[sl][12][...]

    lo = lax.broadcasted_iota(jnp.int32, (CHUNK, PAIR), 1) < R_HEAD
    rt = lax.broadcasted_iota(jnp.int32, (CHUNK, PAIR), 0)
    ct = lax.broadcasted_iota(jnp.int32, (CHUNK, PAIR), 1) % CHUNK
    before = [ct < rt, ct > rt]
    upto = [ct <= rt, ct >= rt]
    same_head = (lax.broadcasted_iota(jnp.int32, (PAIR, PAIR), 0) // R_HEAD
                 == lax.broadcasted_iota(jnp.int32, (PAIR, PAIR), 1) // R_HEAD)
    n_sq = CHUNK.bit_length() - 2
    n_chunks = TILE // CHUNK

    def diag(x):
        x = x.astype(BF16)
        zero = jnp.zeros_like(x)
        return jnp.concatenate([jnp.where(lo, x, zero), jnp.where(lo, zero, x)], 0)

    def chunk(c, carry):
        chains = []
        for sl in range(2):
            for d in range(2):
                al, be, ka, rh, v, wl = ins[sl][6 * d:6 * (d + 1)]
                y_ref = outs[sl][d]
                ce = c if d == 0 else n_chunks - 1 - c
                rows = pl.ds(pl.multiple_of(ce * CHUNK, CHUNK), CHUNK)
                wl_all = wl[pl.ds(n_chunks * d + ce, 1), :]
                for p in range(N_PAIRS):
                    lanes = slice(PAIR * p, PAIR * (p + 1))
                    chains.append(dict(
                        sl=sl, d=d, p=p, rows=rows, lanes=lanes, y_ref=y_ref,
                        wl=wl_all[:, lanes], a=al[rows, lanes], b=be[rows, lanes],
                        k=ka[rows, lanes], r=rh[rows, lanes], v=v[rows, lanes],
                        s=s_scr[sl, d, p]))

        zero = jnp.zeros((CHUNK, PAIR), F32)
        for ch in chains:
            ar = jnp.concatenate([ch["a"], ch["r"]], 0)
            ch["v_d"] = diag(ch["v"])
            bks = jnp.concatenate([diag(ch["b"]), diag(ch["k"]), ch["s"].astype(BF16)], 0)
            ch["gram"] = lax.dot_general(ar, bks, _NT, preferred_element_type=F32)
        for ch in chains:
            g, d = ch["gram"], ch["d"]
            ch["m_ab"] = jnp.where(before[d], g[:CHUNK, :PAIR], zero)
            ch["m_rb"] = jnp.where(upto[d], g[CHUNK:, :PAIR], zero)
            m_ak = jnp.where(before[d], g[:CHUNK, PAIR:2 * PAIR], zero)
            m_rk = jnp.where(upto[d], g[CHUNK:, PAIR:2 * PAIR], zero)
            ch["mv"] = _mm(jnp.concatenate([m_ak, m_rk], 0), ch["v_d"])

        for ch in chains:
            ch["x"] = ch["gram"][:CHUNK, 2 * PAIR:] + ch["mv"][:CHUNK]
            ch["pw"] = ch["m_ab"]
        for lvl in range(n_sq + 1):
            for ch in chains:
                if lvl < n_sq:
                    px = _mm(ch["pw"], jnp.concatenate([diag(ch["pw"]), diag(ch["x"])], 1))
                    ch["pw"] = px[:, :PAIR]
                    step = px[:, PAIR:]
                else:
                    step = _mm(ch["pw"], diag(ch["x"]))
                ch["x"] = ch["x"] - step if lvl == 0 else ch["x"] + step

        for ch in chains:
            z = -ch["x"]
            ch["y"] = ch["gram"][CHUNK:, 2 * PAIR:] + ch["mv"][CHUNK:] + _mm(ch["m_rb"], diag(z))
            ch["zv_t"] = jnp.concatenate([z, ch["v"].astype(F32)], 0).T
        for ch in chains:
            bk = jnp.concatenate([ch["b"], ch["k"]], 0)
            upd = jnp.where(same_head, _mm(ch["zv_t"], bk), 0.0)
            ch["s_new"] = (ch["s"] + upd) * ch["wl"]
        for ch in chains:
            ch["y_ref"][ch["rows"], ch["lanes"]] = ch["y"]
            s_scr[ch["sl"], ch["d"], ch["p"]] = ch["s_new"]
        return carry

    lax.fori_loop(0, n_chunks, chunk, 0)
    for sl in range(2):
        st_ref = outs[sl][2]
        for d in range(2):
            for p in range(N_PAIRS):
                s = s_scr[sl, d, p]
                st_ref[d, 2 * p] = s[:R_HEAD, :R_HEAD]
                st_ref[d, 2 * p + 1] = s[R_HEAD:, R_HEAD:]


def _expand_state(s0):
    n = s0.shape[0]
    s = s0.reshape(n, 2, N_PAIRS, 2, R_HEAD, R_HEAD)
    z = jnp.zeros_like(s[:, :, :, 0])
    top = jnp.concatenate([s[:, :, :, 0], z], -1)
    bot = jnp.concatenate([z, s[:, :, :, 1]], -1)
    return jnp.concatenate([top, bot], -2)


def _scan(al, be, ka, rh, v, wl, s0_lat, n_ctx_tiles, tiles_per_lat):
    nt = v.shape[0]
    n_tiles = nt // TILE
    half = n_tiles // 2
    n_lat = s0_lat.shape[0]
    assert n_tiles % 2 == 0 and (half <= n_ctx_tiles or (half - n_ctx_tiles) % tiles_per_lat == 0)

    def mirror(t):
        j = (t - n_ctx_tiles) % tiles_per_lat
        return jnp.where(t < n_ctx_tiles, t, t - j + (tiles_per_lat - 1 - j))

    lat_seq = lambda t: jnp.clip((t - n_ctx_tiles) // tiles_per_lat, 0, n_lat - 1)
    state_in = (None, 2, N_PAIRS, PAIR, PAIR)
    state_out = (None, 2, R_HEADS, R_HEAD, R_HEAD)
    in_specs, out_specs, out_shape, n_ctx_slot = [], [], [], []
    for sl in range(2):
        t0 = sl * half
        fwd3 = pl.BlockSpec((None, TILE, RW), lambda i, t0=t0: (0, t0 + i, 0))
        bwd3 = pl.BlockSpec((None, TILE, RW), lambda i, t0=t0: (1, mirror(t0 + i), 0))
        fwd2 = pl.BlockSpec((TILE, RW), lambda i, t0=t0: (t0 + i, 0))
        bwd2 = pl.BlockSpec((TILE, RW), lambda i, t0=t0: (mirror(t0 + i), 0))
        wl_f = pl.BlockSpec((None, 8, RW), lambda i, t0=t0: (t0 + i, 0, 0))
        wl_b = pl.BlockSpec((None, 8, RW), lambda i, t0=t0: (mirror(t0 + i), 0, 0))
        in_specs += [fwd3, fwd3, fwd3, fwd3, fwd2, wl_f, bwd3, bwd3, bwd3, bwd3, bwd2, wl_b,
                     pl.BlockSpec(state_in, lambda i, t0=t0: (lat_seq(t0 + i), 0, 0, 0, 0))]
        n_c = min(max(n_ctx_tiles - t0, 0), half)
        n_park = 1 if n_c < half else 0
        n_ctx_slot.append(n_c)
        out_specs += [pl.BlockSpec((TILE, RW), lambda i: (i, 0)),
                      pl.BlockSpec((TILE, RW), lambda i, t0=t0: (mirror(t0 + i) - t0, 0)),
                      pl.BlockSpec(state_out, lambda i, n_c=n_c: (jnp.minimum(i, n_c), 0, 0, 0, 0))]
        out_shape += [jax.ShapeDtypeStruct((half * TILE, RW), F32),
                      jax.ShapeDtypeStruct((half * TILE, RW), F32),
                      jax.ShapeDtypeStruct((n_c + n_park, 2, R_HEADS, R_HEAD, R_HEAD), F32)]
    kern = functools.partial(_scan_kernel, n_ctx_tiles=n_ctx_tiles, tiles_per_lat=tiles_per_lat,
                             half=half)
    s0 = _expand_state(s0_lat)
    args = (al, be, ka, rh, v, wl, al, be, ka, rh, v, wl, s0)
    ya0, ya1, st_a, yb0, yb1, st_b = pl.pallas_call(
        kern,
        grid=(half,),
        in_specs=in_specs,
        out_specs=out_specs,
        out_shape=out_shape,
        scratch_shapes=[pltpu.VMEM((2, 2, N_PAIRS, PAIR, PAIR), F32)],
        compiler_params=_params(0, 1),
        name="wkv_scan",
    )(*args, *args)
    st = jnp.concatenate([st_a[:n_ctx_slot[0]], st_b[:n_ctx_slot[1]]], 0)
    return (ya0, yb0), (ya1, yb1), st


def _dup_halves(x, lo):
    xr = pltpu.roll(x, HEAD_DIM, 1)
    return jnp.where(lo, x, xr), jnp.where(lo, xr, x)


def _stack_heads(qp, lo):
    zero = jnp.zeros_like(qp)
    return jnp.concatenate([jnp.where(lo, qp, zero), jnp.where(lo, zero, qp)], 0)


def _sink_row(sink_ref, p, rows):
    c = lax.broadcasted_iota(jnp.int32, (1, 2 * rows), 1)
    return jnp.where(c < rows, sink_ref[2 * p:2 * p + 1, 0:1], sink_ref[2 * p + 1:2 * p + 2, 0:1])


def _softmax_pv(s_t, sk, v_t):
    m = jnp.maximum(jnp.max(s_t, 0, keepdims=True), sk)
    e = jnp.exp(s_t - m)
    den = jnp.sum(e, 0, keepdims=True) + jnp.exp(sk - m)
    return jnp.dot(v_t, e.astype(BF16), preferred_element_type=F32) / den


def _unstack_heads_t(o_t, rows):
    top = lax.broadcasted_iota(jnp.int32, (PAIR, rows), 0) < HEAD_DIM
    return jnp.where(top, o_t[:, :rows], o_t[:, rows:]).T


def _attn_ctx_kernel(q_ref, kv_ref, sink_ref, o_ref):
    rows = q_ref.shape[0]
    lo = lax.broadcasted_iota(jnp.int32, (rows, PAIR), 1) < HEAD_DIM
    k_dup = [k.astype(BF16) for k in _dup_halves(kv_ref[:, :ATT_KV], lo)]
    v_t = [v.T.astype(BF16) for v in _dup_halves(kv_ref[:, ATT_KV:], lo)]
    scale = HEAD_DIM ** -0.5
    pairs = range(N_HEADS // 2)
    kv_head = [(2 * p) // (N_HEADS // KV_HEADS) for p in pairs]
    qs = [_stack_heads(q_ref[:, PAIR * p:PAIR * (p + 1)] * scale, lo).astype(BF16) for p in pairs]
    s_t = [lax.dot_general(k_dup[kv_head[p]], qs[p], _NT, preferred_element_type=F32)
           for p in pairs]
    o_t = [_softmax_pv(s_t[p], _sink_row(sink_ref, p, rows), v_t[kv_head[p]]) for p in pairs]
    for p in pairs:
        o_ref[:, PAIR * p:PAIR * (p + 1)] = _unstack_heads_t(o_t[p], rows)


def _attn_ctx(q, kv, sink_rows, layer, n_seq, seq_len):
    return pl.pallas_call(
        _attn_ctx_kernel,
        grid=(n_seq,),
        in_specs=[pl.BlockSpec((seq_len, ATT_Q), lambda b: (b, 0)),
                  pl.BlockSpec((seq_len, 2 * ATT_KV), lambda b: (b, 0)),
                  _at_layer((N_HEADS, 128), layer)],
        out_specs=pl.BlockSpec((seq_len, ATT_Q), lambda b: (b, 0)),
        out_shape=jax.ShapeDtypeStruct((n_seq * seq_len, ATT_Q), F32),
        compiler_params=_params(1),
        name="attn_ctx",
    )(q, kv, sink_rows)


def _rope(x, cos, sgn_sin, first16):
    sw = jnp.where(first16, pltpu.roll(x, PAIR - 16, 1), pltpu.roll(x, 16, 1))
    return x * cos + sw * sgn_sin


def _attn_lat_kernel(q_ref, kv_ref, ck_ref, cv_ref, cos_ref, sin_ref, sink_ref, o_ref,
                     kd_scr, vt_scr, ckd_scr, cvt_scr, *, seq_len):
    i = pl.program_id(1)
    qb = q_ref.shape[0]
    n_blk = seq_len // qb

    @pl.when(i == 0)
    def _():
        lo_s = lax.broadcasted_iota(jnp.int32, (seq_len, PAIR), 1) < HEAD_DIM
        f16_s = (lax.broadcasted_iota(jnp.int32, (seq_len, PAIR), 1) % 32) < 16
        k_r = _rope(kv_ref[:, :ATT_KV], cos_ref[...], sin_ref[...], f16_s)
        k_dup = _dup_halves(k_r, lo_s)
        v_dup = _dup_halves(kv_ref[:, ATT_KV:], lo_s)
        lo_c = lax.broadcasted_iota(jnp.int32, ck_ref.shape[1:], 1) < HEAD_DIM
        ck_dup = _dup_halves(ck_ref[0], lo_c)
        cv_dup = _dup_halves(cv_ref[0], lo_c)
        for g in range(KV_HEADS):
            for b in range(n_blk):
                kd_scr[g, b] = k_dup[g][b * qb:(b + 1) * qb].astype(BF16)
                vt_scr[g, b] = v_dup[g][b * qb:(b + 1) * qb].T.astype(BF16)
            ckd_scr[g] = ck_dup[g].astype(BF16)
            cvt_scr[g] = cv_dup[g].T.astype(BF16)

    lo = lax.broadcasted_iota(jnp.int32, (qb, PAIR), 1) < HEAD_DIM
    f16 = (lax.broadcasted_iota(jnp.int32, (qb, PAIR), 1) % 32) < 16
    q0 = pl.multiple_of(i * qb, qb)
    cos = cos_ref[pl.ds(q0, qb), :]
    sin = sin_ref[pl.ds(q0, qb), :]
    qpos = i * qb + lax.broadcasted_iota(jnp.int32, (qb, 2 * qb), 1) % qb
    krow = lax.broadcasted_iota(jnp.int32, (qb, 2 * qb), 0)
    scale = HEAD_DIM ** -0.5

    pairs = range(N_HEADS // 2)
    kv_head = [(2 * p) // (N_HEADS // KV_HEADS) for p in pairs]
    win = []
    for jj in range(3):
        j = i - 1 + jj
        jc = jnp.clip(j, 0, n_blk - 1)
        kpos = jnp.where(j == jc, jc * qb, -4 * seq_len) + krow
        win.append((jc, jnp.abs(qpos - kpos) <= WINDOW))

    qs = [_stack_heads(_rope(q_ref[:, PAIR * p:PAIR * (p + 1)], cos, sin, f16) * scale,
                       lo).astype(BF16) for p in pairs]
    s_t = []
    for p in pairs:
        g = kv_head[p]
        nt = functools.partial(lax.dot_general, dimension_numbers=_NT, preferred_element_type=F32)
        parts = [nt(ckd_scr[g], qs[p])]
        for jc, ok in win:
            parts.append(jnp.where(ok, nt(kd_scr[g, jc], qs[p]), NEG_INF))
        s_t.append(jnp.concatenate(parts, 0))
    o_t = []
    for p in pairs:
        g = kv_head[p]
        v_t = jnp.concatenate([cvt_scr[g]] + [vt_scr[g, jc] for jc, _ in win], 1)
        o_t.append(_softmax_pv(s_t[p], _sink_row(sink_ref, p, qb), v_t))
    for p in pairs:
        o_ref[:, PAIR * p:PAIR * (p + 1)] = _unstack_heads_t(o_t[p], qb)


def _attn_lat(q, kv, ck, cv, layer, cos_t, sin_t, sink_rows, row0, n_seq, seq_len):
    qb = WINDOW
    n_blk = seq_len // qb
    blk0 = row0 // qb
    seq0 = row0 // seq_len
    n_past = ck.shape[2]
    kern = functools.partial(_attn_lat_kernel, seq_len=seq_len)
    return pl.pallas_call(
        kern,
        grid=(n_seq, n_blk),
        in_specs=[pl.BlockSpec((qb, ATT_Q), lambda b, i: (blk0 + b * n_blk + i, 0)),
                  pl.BlockSpec((seq_len, 2 * ATT_KV), lambda b, i: (seq0 + b, 0)),
                  pl.BlockSpec((1, None, n_past, ATT_KV), lambda b, i: (b, layer, 0, 0)),
                  pl.BlockSpec((1, None, n_past, ATT_KV), lambda b, i: (b, layer, 0, 0)),
                  pl.BlockSpec((seq_len, PAIR), lambda b, i: (0, 0)),
                  pl.BlockSpec((seq_len, PAIR), lambda b, i: (0, 0)),
                  _at_layer((N_HEADS, 128), layer)],
        out_specs=pl.BlockSpec((qb, ATT_Q), lambda b, i: (b * n_blk + i, 0)),
        out_shape=jax.ShapeDtypeStruct((n_seq * seq_len, ATT_Q), F32),
        scratch_shapes=[pltpu.VMEM((KV_HEADS, n_blk, qb, PAIR), BF16),
                        pltpu.VMEM((KV_HEADS, n_blk, PAIR, qb), BF16),
                        pltpu.VMEM((KV_HEADS, n_past, PAIR), BF16),
                        pltpu.VMEM((KV_HEADS, PAIR, n_past), BF16)],
        compiler_params=_params(1, 1),
        name="attn_lat",
    )(q, kv, ck, cv, cos_t, sin_t, sink_rows)


def _postmix_kernel(x_ref, octx_ref, olat_ref, ya0_ref, yb0_ref, ya1_ref, yb1_ref, bonus_ref,
                    g_ref, mod_ref,
                    gpost_ref, gpre_ref, lng_ref, lnb_ref, wo_ref, bd_ref, router_ref,
                    x1_ref, h_ref, gates_ref, *, with_router, n_ctx_tiles, half):
    m = mod_ref[0]
    bd = bd_ref[...]
    tile = lax.broadcasted_iota(jnp.int32, octx_ref.shape, 0) * 0 + pl.program_id(0)
    o_att = jnp.where(tile < n_ctx_tiles, octx_ref[...], olat_ref[...])
    in_a = (lax.broadcasted_iota(jnp.int32, ya0_ref.shape, 0) * 0 + pl.program_id(0)) < half
    y = jnp.where(in_a, ya0_ref[...] + ya1_ref[...], yb0_ref[...] + yb1_ref[...])
    inv_n = 1.0 / R_HEAD
    mu = _head_sum(y, bd) * inv_n
    yc = y - mu
    var = _head_sum(yc * yc, bd) * inv_n
    yn = yc * lax.rsqrt(var + GN_EPS)
    o_rw = (yn * lng_ref[...] + lnb_ref[...] + bonus_ref[...]) * g_ref[...]
    o = (jnp.dot(o_att.astype(BF16), wo_ref[:ATT_Q, :], preferred_element_type=F32)
         + jnp.dot(o_rw.astype(BF16), wo_ref[ATT_Q:, :], preferred_element_type=F32))
    x1 = x_ref[...] + m[2:3] * (_rms(o) * gpost_ref[...])
    x1_ref[...] = x1
    h = (_rms(x1) * gpre_ref[...]) * (1.0 + m[4:5]) + m[3:4]
    h_ref[...] = h.astype(BF16)
    if with_router:
        h_hi, h_lo = _split2(h)
        r_hi, r_lo = _split2(router_ref[...])
        d = functools.partial(jnp.dot, preferred_element_type=F32)
        logits = d(h_hi, r_hi) + (d(h_hi, r_lo) + d(h_lo, r_hi))
        lane = lax.broadcasted_iota(jnp.int32, logits.shape, 1)
        valid = lane < N_EXPERTS
        logits = jnp.where(valid, logits, NEG_INF)
        e = jnp.exp(logits - jnp.max(logits, -1, keepdims=True))
        probs = e / jnp.sum(e, -1, keepdims=True)
        lane_f = lane.astype(F32)
        p1 = jnp.max(probs, -1, keepdims=True)
        i1 = jnp.min(jnp.where(probs == p1, lane_f, 1e9), -1, keepdims=True)
        rest = jnp.where(lane_f == i1, -1.0, probs)
        p2 = jnp.max(rest, -1, keepdims=True)
        i2 = jnp.min(jnp.where(rest == p2, lane_f, 1e9), -1, keepdims=True)
        sel = (lane_f == i1) | (lane_f == i2)
        gates_ref[...] = jnp.where(sel, probs / (p1 + p2), 0.0)
    else:
        gates_ref[...] = jnp.ones(gates_ref.shape, F32)


def _postmix(x, o_ctx, o_lat, y0, y1, bonus, g, mods, p, w_o, layer, consts, mod_map,
             router_pad, router_layer, n_ctx_tiles):
    nt = x.shape[0]
    row = lambda i: (i, 0)
    c2 = lambda i: (0, 0)
    with_router = router_layer is not None
    half = nt // TILE // 2
    kern = functools.partial(_postmix_kernel, with_router=with_router, n_ctx_tiles=n_ctx_tiles,
                             half=half)
    n_lat_tiles = nt // TILE - n_ctx_tiles
    first_half = pl.BlockSpec((TILE, RW), lambda i: (jnp.minimum(i, half - 1), 0))
    second_half = pl.BlockSpec((TILE, RW), lambda i: (jnp.maximum(i - half, 0), 0))
    return pl.pallas_call(
        kern,
        grid=(nt // TILE,),
        in_specs=[pl.BlockSpec((TILE, D_MODEL), row),
                  pl.BlockSpec((TILE, ATT_Q), lambda i: (jnp.minimum(i, n_ctx_tiles - 1), 0)),
                  pl.BlockSpec((TILE, ATT_Q),
                               lambda i: (jnp.clip(i - n_ctx_tiles, 0, n_lat_tiles - 1), 0)),
                  first_half, second_half, first_half, second_half,
                  pl.BlockSpec((TILE, RW), row),
                  pl.BlockSpec((TILE, RW), row),
                  pl.BlockSpec((None, 1, 6, D_MODEL), lambda i: (layer, mod_map(i), 0, 0)),
                  _at_layer((1, D_MODEL), layer),
                  _at_layer((1, D_MODEL), layer),
                  _at_layer((1, RW), layer),
                  _at_layer((1, RW), layer),
                  _at_layer((ATT_Q + RW, D_MODEL), layer),
                  pl.BlockSpec((RW, RW), c2),
                  _at_layer((D_MODEL, 128), router_layer or 0)],
        out_specs=[pl.BlockSpec((TILE, D_MODEL), row),
                   pl.BlockSpec((TILE, D_MODEL), row),
                   pl.BlockSpec((TILE, 128), row)],
        out_shape=[jax.ShapeDtypeStruct((nt, D_MODEL), F32),
                   jax.ShapeDtypeStruct((nt, D_MODEL), BF16),
                   jax.ShapeDtypeStruct((nt, 128), F32)],
        compiler_params=_params(1),
        name="postmix",
    )(x, o_ctx, o_lat, y0[0], y0[1], y1[0], y1[1], bonus, g, mods, p["g_post_mix"],
      p["g_pre_ffn"],
      p["ln_g"], p["ln_b"], w_o, consts["bd"], router_pad)


def _ffn_kernel(h_ref, x1_ref, gates_ref, mod_ref, gpost_ref, w1_ref, w3_ref, w2_ref, o_ref,
                *, n_e, n_f):
    e = pl.program_id(1)
    f = pl.program_id(2)
    first = (e == 0) & (f == 0)
    last = (e == n_e - 1) & (f == n_f - 1)

    @pl.when(first)
    def _():
        o_ref[...] = jnp.zeros_like(o_ref)

    h = h_ref[...]
    lane = lax.broadcasted_iota(jnp.int32, gates_ref.shape, 1)
    gate = jnp.sum(jnp.where(lane == e, gates_ref[...], 0.0), -1, keepdims=True)
    a = jnp.dot(h, w1_ref[0], preferred_element_type=F32)
    b = jnp.dot(h, w3_ref[0], preferred_element_type=F32)
    t = (a * _sigmoid(a)) * b
    o_ref[...] += gate * jnp.dot(t.astype(BF16), w2_ref[0], preferred_element_type=F32)

    @pl.when(last)
    def _():
        m = mod_ref[0]
        o_ref[...] = x1_ref[...] + m[5:6] * (_rms(o_ref[...]) * gpost_ref[...])


def _ffn(h, x1, gates, mods, g_post, w1, w3, w2, layer, w_layer, mod_map_ffn):
    nt = h.shape[0]
    _, n_e, _, d_ff = w1.shape
    n_f = d_ff // FF_BLOCK
    row = lambda i, e, f: (i, 0)
    return pl.pallas_call(
        functools.partial(_ffn_kernel, n_e=n_e, n_f=n_f),
        grid=(nt // FFN_TILE, n_e, n_f),
        in_specs=[pl.BlockSpec((FFN_TILE, D_MODEL), row),
                  pl.BlockSpec((FFN_TILE, D_MODEL), row),
                  pl.BlockSpec((FFN_TILE, 128), row),
                  pl.BlockSpec((None, 1, 6, D_MODEL),
                               lambda i, e, f: (layer, mod_map_ffn(i), 0, 0)),
                  _at_layer((1, D_MODEL), layer),
                  pl.BlockSpec((None, 1, D_MODEL, FF_BLOCK), lambda i, e, f: (w_layer, e, 0, f)),
                  pl.BlockSpec((None, 1, D_MODEL, FF_BLOCK), lambda i, e, f: (w_layer, e, 0, f)),
                  pl.BlockSpec((None, 1, FF_BLOCK, D_MODEL), lambda i, e, f: (w_layer, e, f, 0))],
        out_specs=pl.BlockSpec((FFN_TILE, D_MODEL), row),
        out_shape=jax.ShapeDtypeStruct((nt, D_MODEL), F32),
        compiler_params=_params(1, 2),
        name="ffn",
    )(h, x1, gates, mods, g_post, w1, w3, w2)


def _rope_tables(seq_len):
    half = HEAD_DIM // 2
    nf = half // 2
    inv = ROPE_THETA ** (-jnp.arange(nf, dtype=F32) / nf)
    t = jnp.arange(seq_len)
    row = (t // GRID_W).astype(F32)
    col = (t % GRID_W).astype(F32)
    lane = jnp.arange(PAIR)
    pos = jnp.where(((lane % HEAD_DIM) // half == 0)[None, :], row[:, None], col[:, None])
    ang = pos * inv[lane % nf][None, :]
    first = ((lane % half) < nf)[None, :]
    return jnp.cos(ang), jnp.where(first, -jnp.sin(ang), jnp.sin(ang))


def _constants():
    r = jnp.arange(RW)
    bd = (r[:, None] // R_HEAD == r[None, :] // R_HEAD).astype(BF16)
    t = jnp.arange(TILE)
    same = t[:, None] // CHUNK == t[None, :] // CHUNK
    tri = jnp.stack([same & (t[None, :] <= t[:, None]), same & (t[None, :] >= t[:, None])])
    return {"bd": bd, "tri": tri.astype(BF16)}


def _pad_rows(w2):
    z = jnp.zeros_like(w2[:, 0])
    return jnp.stack([jnp.concatenate([w2[:, 0], z], 1), jnp.concatenate([z, w2[:, 1]], 1)],
                     1).astype(BF16)


def kernel(x_prompt, x_sample, cache_k, cache_v, state_rwkv, c, c_ctx, w_mod, b_mod, g_pre_mix, g_post_mix, g_pre_ffn, g_post_ffn, w_in, mu_shift, w_o, attn_sink, rw_w0, rw_w1, rw_w2, rw_a0, rw_a1, rw_a2, rw_g1, rw_g2, rw_k_k, rw_k_a, rw_r_k, rw_ln_g, rw_ln_b, ffn_w1, ffn_w3, ffn_w2, moe_router, moe_w1, moe_w3, moe_w2):
    n_ctx, s_ctx, _ = x_prompt.shape
    n_lat, s_lat, _ = x_sample.shape
    n_past = cache_k.shape[2]
    assert s_ctx == TILE and s_lat % TILE == 0 and (n_ctx * s_ctx) % s_lat == 0
    nt_ctx = n_ctx * s_ctx
    n_ctx_tiles = nt_ctx // TILE
    tiles_per_lat = s_lat // TILE
    mod_map = functools.partial(_mod_row, n_ctx_tiles=n_ctx_tiles, tiles_per_lat=tiles_per_lat)
    ffn_per = FFN_TILE // TILE
    mod_map_ffn = lambda i: mod_map(i * ffn_per)

    x = jnp.concatenate([x_prompt.reshape(nt_ctx, D_MODEL), x_sample.reshape(n_lat * s_lat, D_MODEL)], 0)
    n_cond = 8
    cond = jnp.concatenate([c_ctx[None, :], c, jnp.zeros((n_cond - 1 - n_lat, D_MODEL), F32)], 0)
    mods = _modulation(cond, w_mod, b_mod).reshape(DEPTH, n_cond, 6, D_MODEL)

    consts = _constants()
    cos_t, sin_t = _rope_tables(s_lat)
    w_cat = jnp.concatenate(
        [w_in, rw_w1[:, 0], rw_w1[:, 1], rw_a1[:, 0], rw_a1[:, 1], rw_g1], 2).astype(BF16)
    w_o_b = w_o.astype(BF16)
    dense_w = [w.astype(BF16)[:, None] for w in (ffn_w1, ffn_w3, ffn_w2)]
    moe_w = [w.astype(BF16) for w in (moe_w1, moe_w3, moe_w2)]
    router_pad = jnp.pad(moe_router, ((0, 0), (0, 0), (0, 128 - N_EXPERTS)))
    sink_rows = jnp.broadcast_to(attn_sink[:, :, None], (DEPTH, N_HEADS, 128))
    ck = cache_k.reshape(n_lat, DEPTH, n_past, ATT_KV)
    cv = cache_v.reshape(n_lat, DEPTH, n_past, ATT_KV)

    vecs = lambda a: a.reshape(DEPTH, 1, -1)
    p = {"mu": vecs(mu_shift), "w0": rw_w0, "a0": rw_a0, "w2": _pad_rows(rw_w2),
         "a2": _pad_rows(rw_a2), "g2": rw_g2.astype(BF16), "k_k": vecs(rw_k_k),
         "k_a": vecs(rw_k_a), "r_k": vecs(rw_r_k), "ln_g": vecs(rw_ln_g), "ln_b": vecs(rw_ln_b),
         "g_post_mix": vecs(g_post_mix), "g_pre_ffn": vecs(g_pre_ffn)}
    g_pre_mix_v = vecs(g_pre_mix)
    g_post_ffn_v = vecs(g_post_ffn)

    new_k, new_v, new_s = [], [], []
    for l in range(DEPTH):
        q, kv, v_b, al, be, ka, rh, wl, bonus, g = _inprep(
            x, mods, g_pre_mix_v, w_cat, p, l, consts, mod_map, n_ctx_tiles, tiles_per_lat)
        y0, y1, s_ctx_new = _scan(al, be, ka, rh, v_b, wl, state_rwkv[:, l],
                                  n_ctx_tiles, tiles_per_lat)
        o_ctx = _attn_ctx(q, kv, sink_rows, l, n_ctx, s_ctx)
        o_lat = _attn_lat(q, kv, ck, cv, l, cos_t, sin_t, sink_rows, nt_ctx, n_lat, s_lat)

        i = l // 2
        moe = l % 2 == 1
        x1, h, gates = _postmix(x, o_ctx, o_lat, y0, y1, bonus, g, mods, p, w_o_b, l, consts,
                                mod_map, router_pad, i if moe else None, n_ctx_tiles)
        w1, w3, w2 = moe_w if moe else dense_w
        x = _ffn(h, x1, gates, mods, g_post_ffn_v, w1, w3, w2, l, i, mod_map_ffn)

        new_k.append(kv[:nt_ctx, :ATT_KV].reshape(n_ctx, s_ctx, KV_HEADS, HEAD_DIM))
        new_v.append(kv[:nt_ctx, ATT_KV:].reshape(n_ctx, s_ctx, KV_HEADS, HEAD_DIM))
        new_s.append(s_ctx_new)

    y_prompt = x[:nt_ctx].reshape(n_ctx, s_ctx, D_MODEL)
    y_sample = x[nt_ctx:].reshape(n_lat, s_lat, D_MODEL)
    return (y_prompt, y_sample, jnp.stack(new_k, 1), jnp.stack(new_v, 1), jnp.stack(new_s, 1))
```

```python
import functools

import jax
import jax.numpy as jnp
from jax import lax
from jax.experimental import pallas as pl
from jax.experimental.pallas import tpu as pltpu

F32 = jnp.float32
BF16 = jnp.bfloat16

D_MODEL = 1024
DEPTH = 4
GRID_W = 64
N_HEADS = 8
KV_HEADS = 2
HEAD_DIM = 64
WINDOW = 128
ROPE_THETA = 10000.0
R_HEADS = 8
R_HEAD = 64
RW = R_HEADS * R_HEAD
LORA_W = 64
LORA_A = 64
LORA_G = 128
LORA_ALL = 2 * LORA_W + 2 * LORA_A + LORA_G
ATT_Q = N_HEADS * HEAD_DIM
ATT_KV = KV_HEADS * HEAD_DIM
D_IN = ATT_Q + 2 * ATT_KV + 3 * RW
D_FF = 2816
N_EXPERTS = 8
D_FF_EXPERT = 1408
NORM_EPS = 1e-6
GN_EPS = 64e-5
NEG_INF = -1e30

TILE = 256
CHUNK = 64
PAIR = 2 * R_HEAD
N_PAIRS = R_HEADS // 2
SCAN_SLOTS = 3
FFN_TILE = 512
FF_SUB = 1024
MOD_COLS = 1536
VMEM_LIMIT = 48 * 1024 * 1024
FFN_VMEM_LIMIT = 58 * 1024 * 1024

_NT = (((1,), (1,)), ((), ()))


def _params(n_parallel, n_arbitrary=0, vmem_limit=VMEM_LIMIT):
    sem = ("parallel",) * n_parallel + ("arbitrary",) * n_arbitrary
    return pltpu.CompilerParams(dimension_semantics=sem, vmem_limit_bytes=vmem_limit)


def _sigmoid(x):
    return 0.5 * jnp.tanh(0.5 * x) + 0.5


def _mm(a, b):
    return jnp.dot(a.astype(BF16), b.astype(BF16), preferred_element_type=F32)


def _mm_nt(a, b):
    return lax.dot_general(a.astype(BF16), b.astype(BF16), _NT, preferred_element_type=F32)


def _split2(x):
    hi = x.astype(BF16)
    return hi, (x - hi.astype(F32)).astype(BF16)


def _head_sum(x, ones_bf16):
    return jnp.dot(x.astype(BF16), ones_bf16, preferred_element_type=F32)


def _mm_ones_left(ones_bf16, x):
    hi, lo = _split2(x)
    d = functools.partial(jnp.dot, preferred_element_type=F32)
    return d(ones_bf16, hi) + d(ones_bf16, lo)


def _rms(x):
    return x * lax.rsqrt(jnp.mean(x * x, -1, keepdims=True) + NORM_EPS)


def _at_layer(shape, layer):
    zeros = (0,) * len(shape)
    return pl.BlockSpec((None,) + tuple(shape), lambda *_: (layer,) + zeros)


def _mod_row(i, n_ctx_tiles, tiles_per_lat):
    return jnp.where(i < n_ctx_tiles, 0, 1 + (i - n_ctx_tiles) // tiles_per_lat)


def _mod_kernel(cond_ref, w_ref, b_ref, o_ref):
    c = cond_ref[...]
    s = c * _sigmoid(c)
    o_ref[0] = _mm(s, w_ref[0]) + b_ref[0]


def _modulation(cond, w_mod, b_mod):
    n_cond = cond.shape[0]
    n_col = 6 * D_MODEL // MOD_COLS
    return pl.pallas_call(
        _mod_kernel,
        grid=(DEPTH, n_col),
        in_specs=[pl.BlockSpec((n_cond, D_MODEL), lambda l, j: (0, 0)),
                  pl.BlockSpec((1, D_MODEL, MOD_COLS), lambda l, j: (l, 0, j)),
                  pl.BlockSpec((1, 1, MOD_COLS), lambda l, j: (l, 0, j))],
        out_specs=pl.BlockSpec((1, n_cond, MOD_COLS), lambda l, j: (l, 0, j)),
        out_shape=jax.ShapeDtypeStruct((DEPTH, n_cond, 6 * D_MODEL), F32),
        compiler_params=_params(2),
        name="modulation",
    )(cond, w_mod, b_mod.reshape(DEPTH, 1, 6 * D_MODEL))


HALO = 8


def _inprep_kernel(x_ref, xp_ref, xn_ref, mod_ref, g_ref, w_ref, mu_ref, w0_ref, w2_ref, a0_ref,
                   a2_ref, g2_ref, kk_ref, ka_ref, rk_ref, bd_ref, tri_ref,
                   q_out, kv_out, v_out, al_out, be_out, ka_out, rh_out, wl_out, bonus_out, g_out,
                   *, n_ctx_tiles, tiles_per_lat):
    i = pl.program_id(0)
    j = (i - n_ctx_tiles) % tiles_per_lat
    is_ctx = i < n_ctx_tiles
    first = is_ctx | (j == 0)
    last = is_ctx | (j == tiles_per_lat - 1)

    m = mod_ref[0]
    x_all = jnp.concatenate([xp_ref[...], x_ref[...], xn_ref[...]], 0)
    h = (_rms(x_all) * g_ref[...]) * (1.0 + m[1:2]) + m[0:1]
    main = slice(HALO, HALO + TILE)
    h_main = h[main].astype(BF16)
    n_att = ATT_Q + 2 * ATT_KV
    u_all = jnp.dot(h.astype(BF16), w_ref[:, n_att:], preferred_element_type=F32)
    rkv = slice(0, D_IN - n_att)
    u = u_all[main, rkv]
    lora = u_all[main, D_IN - n_att:]

    row = lax.broadcasted_iota(jnp.int32, (TILE, 1), 0)
    p_row = u_all[HALO - 1:HALO, rkv] * (1.0 - first.astype(F32))
    n_row = u_all[HALO + TILE:HALO + TILE + 1, rkv] * (1.0 - last.astype(F32))
    prev = jnp.where(row == 0, p_row, pltpu.roll(u, 1, 0))
    nxt = jnp.where(row == TILE - 1, n_row, pltpu.roll(u, TILE - 1, 0))
    us = u + mu_ref[...] * (0.5 * (prev + nxt) - u)
    r = us[:, :RW]
    k = us[:, RW:2 * RW]
    v = us[:, 2 * RW:]

    bd = bd_ref[...]
    kk = k * kk_ref[...]
    kk = kk * lax.rsqrt(_head_sum(kk * kk, bd) + 1e-12)

    lw_in = jnp.tanh(lora[:, :2 * LORA_W])
    la_in = lora[:, 2 * LORA_W:2 * LORA_W + 2 * LORA_A]
    kd_sum = jnp.zeros_like(k)
    for d in range(2):
        w_pre = w0_ref[d:d + 1, :] + _mm(lw_in, w2_ref[d])
        lw = -jnp.exp(-0.5) * _sigmoid(w_pre)
        a = _sigmoid(a0_ref[d:d + 1, :] + _mm(la_in, a2_ref[d]))
        kd = k * (1.0 + (a - 1.0) * ka_ref[...])
        kd_sum = kd_sum + kd
        cw = _mm_ones_left(tri_ref[d], lw)
        e_neg = jnp.exp(-cw)
        al_out[d] = (kk * jnp.exp(cw - lw)).astype(BF16)
        be_out[d] = (kk * a * e_neg).astype(BF16)
        ka_out[d] = (kd * e_neg).astype(BF16)
        rh_out[d] = (r * jnp.exp(cw)).astype(BF16)
        for c in range(TILE // CHUNK):
            end = c * CHUNK + (CHUNK - 1 if d == 0 else 0)
            wl_out[0, 4 * d + c:4 * d + c + 1, :] = jnp.exp(cw[end:end + 1, :])
    v_out[...] = v.astype(BF16)
    bonus_out[...] = _head_sum(r * kd_sum * rk_ref[...], bd) * v
    g_out[...] = _mm(_sigmoid(lora[:, 2 * LORA_W + 2 * LORA_A:]), g2_ref[...])
    u_att = jnp.dot(h_main, w_ref[:, :n_att], preferred_element_type=F32)
    q_out[...] = u_att[:, :ATT_Q]
    kv_out[...] = u_att[:, ATT_Q:]


def _inprep(x, mods, g, w_cat, p, layer, consts, mod_map, n_ctx_tiles, tiles_per_lat):
    nt = x.shape[0]
    n_tiles = nt // TILE
    row = lambda i: (i, 0)
    drow = lambda i: (0, i, 0)
    hb = TILE // HALO
    kern = functools.partial(_inprep_kernel, n_ctx_tiles=n_ctx_tiles, tiles_per_lat=tiles_per_lat)
    dshape = jax.ShapeDtypeStruct((2, nt, RW), BF16)
    return pl.pallas_call(
        kern,
        grid=(n_tiles,),
        in_specs=[pl.BlockSpec((TILE, D_MODEL), row),
                  pl.BlockSpec((HALO, D_MODEL), lambda i: (jnp.maximum(i * hb - 1, 0), 0)),
                  pl.BlockSpec((HALO, D_MODEL),
                               lambda i: (jnp.minimum((i + 1) * hb, nt // HALO - 1), 0)),
                  pl.BlockSpec((None, 1, 6, D_MODEL), lambda i: (layer, mod_map(i), 0, 0)),
                  _at_layer((1, D_MODEL), layer),
                  _at_layer((D_MODEL, D_IN + LORA_ALL), layer),
                  _at_layer((1, 3 * RW), layer),
                  _at_layer((2, RW), layer),
                  _at_layer((2, 2 * LORA_W, RW), layer),
                  _at_layer((2, RW), layer),
                  _at_layer((2, 2 * LORA_A, RW), layer),
                  _at_layer((LORA_G, RW), layer),
                  _at_layer((1, RW), layer),
                  _at_layer((1, RW), layer),
                  _at_layer((1, RW), layer),
                  pl.BlockSpec((RW, RW), lambda i: (0, 0)),
                  pl.BlockSpec((2, TILE, TILE), lambda i: (0, 0, 0))],
        out_specs=[pl.BlockSpec((TILE, ATT_Q), row),
                   pl.BlockSpec((TILE, 2 * ATT_KV), row),
                   pl.BlockSpec((TILE, RW), row),
                   pl.BlockSpec((2, TILE, RW), drow),
                   pl.BlockSpec((2, TILE, RW), drow),
                   pl.BlockSpec((2, TILE, RW), drow),
                   pl.BlockSpec((2, TILE, RW), drow),
                   pl.BlockSpec((1, 8, RW), lambda i: (i, 0, 0)),
                   pl.BlockSpec((TILE, RW), row),
                   pl.BlockSpec((TILE, RW), row)],
        out_shape=[jax.ShapeDtypeStruct((nt, ATT_Q), F32),
                   jax.ShapeDtypeStruct((nt, 2 * ATT_KV), F32),
                   jax.ShapeDtypeStruct((nt, RW), BF16), dshape, dshape, dshape, dshape,
                   jax.ShapeDtypeStruct((n_tiles, 8, RW), F32),
                   jax.ShapeDtypeStruct((nt, RW), F32),
                   jax.ShapeDtypeStruct((nt, RW), F32)],
        compiler_params=_params(1),
        name="inproj_prep",
    )(x, x, x, mods, g, w_cat, p["mu"], p["w0"], p["w2"], p["a0"], p["a2"], p["g2"],
      p["k_k"], p["k_a"], p["r_k"], consts["bd"], consts["tri"])


def _scan_kernel(*refs, n_ctx_tiles, tiles_per_lat, per_slot):
    n_in = 13
    slots = range(SCAN_SLOTS)
    ins = [refs[n_in * sl:n_in * (sl + 1)] for sl in slots]
    outs = [refs[SCAN_SLOTS * n_in + 3 * sl:SCAN_SLOTS * n_in + 3 * (sl + 1)] for sl in slots]
    s_scr = refs[-1]
    for sl in slots:
        tile = sl * per_slot + pl.program_id(0)
        is_ctx = tile < n_ctx_tiles
        seq_start = (tile - n_ctx_tiles) % tiles_per_lat == 0

        @pl.when(is_ctx)
        def _():
            s_scr[sl] = jnp.zeros(s_scr.shape[1:], F32)

        @pl.when(jnp.logical_not(is_ctx) & seq_start)
        def _():
            s_scr[sl] = ins[sl][12][...]

    lo = lax.broadcasted_iota(jnp.int32, (CHUNK, PAIR), 1) < R_HEAD
    rt = lax.broadcasted_iota(jnp.int32, (CHUNK, PAIR), 0)
    ct = lax.broadcasted_iota(jnp.int32, (CHUNK, PAIR), 1) % CHUNK
    before = [ct < rt, ct > rt]
    upto = [ct <= rt, ct >= rt]
    same_head = (lax.broadcasted_iota(jnp.int32, (PAIR, PAIR), 0) // R_HEAD
                 == lax.broadcasted_iota(jnp.int32, (PAIR, PAIR), 1) // R_HEAD)
    n_sq = CHUNK.bit_length() - 2
    n_chunks = TILE // CHUNK

    def diag(x):
        x = x.astype(BF16)
        zero = jnp.zeros_like(x)
        return jnp.concatenate([jnp.where(lo, x, zero), jnp.where(lo, zero, x)], 0)

    def chunk(c, carry):
        chains = []
        for sl in slots:
            for d in range(2):
                al, be, ka, rh, v, wl = ins[sl][6 * d:6 * (d + 1)]
                y_ref = outs[sl][d]
                ce = c if d == 0 else n_chunks - 1 - c
                rows = pl.ds(pl.multiple_of(ce * CHUNK, CHUNK), CHUNK)
                wl_all = wl[pl.ds(n_chunks * d + ce, 1), :]
                for p in range(N_PAIRS):
                    lanes = slice(PAIR * p, PAIR * (p + 1))
                    chains.append(dict(
                        sl=sl, d=d, p=p, rows=rows, lanes=lanes, y_ref=y_ref,
                        wl=wl_all[:, lanes], a=al[rows, lanes], b=be[rows, lanes],
                        k=ka[rows, lanes], r=rh[rows, lanes], v=v[rows, lanes],
                        s=s_scr[sl, d, p]))

        zero = jnp.zeros((CHUNK, PAIR), F32)
        for ch in chains:
            ar = jnp.concatenate([ch["a"], ch["r"]], 0)
            ch["v_d"] = diag(ch["v"])
            bks = jnp.concatenate([diag(ch["b"]), diag(ch["k"]), ch["s"].astype(BF16)], 0)
            ch["gram"] = lax.dot_general(ar, bks, _NT, preferred_element_type=F32)
        for ch in chains:
            g, d = ch["gram"], ch["d"]
            ch["m_ab"] = jnp.where(before[d], g[:CHUNK, :PAIR], zero)
            ch["m_rb"] = jnp.where(upto[d], g[CHUNK:, :PAIR], zero)
            m_ak = jnp.where(before[d], g[:CHUNK, PAIR:2 * PAIR], zero)
            m_rk = jnp.where(upto[d], g[CHUNK:, PAIR:2 * PAIR], zero)
            ch["mv"] = _mm(jnp.concatenate([m_ak, m_rk], 0), ch["v_d"])

        for ch in chains:
            ch["x"] = ch["gram"][:CHUNK, 2 * PAIR:] + ch["mv"][:CHUNK]
            ch["pw"] = ch["m_ab"]
        for lvl in range(n_sq + 1):
            for ch in chains:
                if lvl < n_sq:
                    px = _mm(ch["pw"], jnp.concatenate([diag(ch["pw"]), diag(ch["x"])], 1))
                    ch["pw"] = px[:, :PAIR]
                    step = px[:, PAIR:]
                else:
                    step = _mm(ch["pw"], diag(ch["x"]))
                ch["x"] = ch["x"] - step if lvl == 0 else ch["x"] + step

        for ch in chains:
            z = -ch["x"]
            ch["y"] = ch["gram"][CHUNK:, 2 * PAIR:] + ch["mv"][CHUNK:] + _mm(ch["m_rb"], diag(z))
            ch["zv_t"] = jnp.concatenate([z, ch["v"].astype(F32)], 0).T
        for ch in chains:
            bk = jnp.concatenate([ch["b"], ch["k"]], 0)
            upd = jnp.where(same_head, _mm(ch["zv_t"], bk), 0.0)
            ch["s_new"] = (ch["s"] + upd) * ch["wl"]
        for ch in chains:
            ch["y_ref"][ch["rows"], ch["lanes"]] = ch["y"]
            s_scr[ch["sl"], ch["d"], ch["p"]] = ch["s_new"]
        return carry

    lax.fori_loop(0, n_chunks, chunk, 0)
    for sl in slots:
        st_ref = outs[sl][2]
        for d in range(2):
            for p in range(N_PAIRS):
                s = s_scr[sl, d, p]
                st_ref[d, 2 * p] = s[:R_HEAD, :R_HEAD]
                st_ref[d, 2 * p + 1] = s[R_HEAD:, R_HEAD:]


def _expand_state(s0):
    n = s0.shape[0]
    s = s0.reshape(n, 2, N_PAIRS, 2, R_HEAD, R_HEAD)
    z = jnp.zeros_like(s[:, :, :, 0])
    top = jnp.concatenate([s[:, :, :, 0], z], -1)
    bot = jnp.concatenate([z, s[:, :, :, 1]], -1)
    return jnp.concatenate([top, bot], -2)


def _scan(al, be, ka, rh, v, wl, s0_lat, n_ctx_tiles, tiles_per_lat):
    nt = v.shape[0]
    n_tiles = nt // TILE
    per_slot = n_tiles // SCAN_SLOTS
    n_lat = s0_lat.shape[0]
    assert n_tiles % SCAN_SLOTS == 0
    for t0 in range(per_slot, n_tiles, per_slot):
        assert t0 <= n_ctx_tiles or (t0 - n_ctx_tiles) % tiles_per_lat == 0

    def mirror(t):
        j = (t - n_ctx_tiles) % tiles_per_lat
        return jnp.where(t < n_ctx_tiles, t, t - j + (tiles_per_lat - 1 - j))

    lat_seq = lambda t: jnp.clip((t - n_ctx_tiles) // tiles_per_lat, 0, n_lat - 1)
    state_in = (None, 2, N_PAIRS, PAIR, PAIR)
    state_out = (None, 2, R_HEADS, R_HEAD, R_HEAD)
    in_specs, out_specs, out_shape, n_ctx_slot = [], [], [], []
    for sl in range(SCAN_SLOTS):
        t0 = sl * per_slot
        fwd3 = pl.BlockSpec((None, TILE, RW), lambda i, t0=t0: (0, t0 + i, 0))
        bwd3 = pl.BlockSpec((None, TILE, RW), lambda i, t0=t0: (1, mirror(t0 + i), 0))
        fwd2 = pl.BlockSpec((TILE, RW), lambda i, t0=t0: (t0 + i, 0))
        bwd2 = pl.BlockSpec((TILE, RW), lambda i, t0=t0: (mirror(t0 + i), 0))
        wl_f = pl.BlockSpec((None, 8, RW), lambda i, t0=t0: (t0 + i, 0, 0))
        wl_b = pl.BlockSpec((None, 8, RW), lambda i, t0=t0: (mirror(t0 + i), 0, 0))
        in_specs += [fwd3, fwd3, fwd3, fwd3, fwd2, wl_f, bwd3, bwd3, bwd3, bwd3, bwd2, wl_b,
                     pl.BlockSpec(state_in, lambda i, t0=t0: (lat_seq(t0 + i), 0, 0, 0, 0))]
        n_c = min(max(n_ctx_tiles - t0, 0), per_slot)
        n_park = 1 if n_c < per_slot else 0
        n_ctx_slot.append(n_c)
        out_specs += [pl.BlockSpec((TILE, RW), lambda i: (i, 0)),
                      pl.BlockSpec((TILE, RW), lambda i, t0=t0: (mirror(t0 + i) - t0, 0)),
                      pl.BlockSpec(state_out, lambda i, n_c=n_c: (jnp.minimum(i, n_c), 0, 0, 0, 0))]
        out_shape += [jax.ShapeDtypeStruct((per_slot * TILE, RW), F32),
                      jax.ShapeDtypeStruct((per_slot * TILE, RW), F32),
                      jax.ShapeDtypeStruct((n_c + n_park, 2, R_HEADS, R_HEAD, R_HEAD), F32)]
    kern = functools.partial(_scan_kernel, n_ctx_tiles=n_ctx_tiles, tiles_per_lat=tiles_per_lat,
                             per_slot=per_slot)
    s0 = _expand_state(s0_lat)
    args = (al, be, ka, rh, v, wl, al, be, ka, rh, v, wl, s0)
    outs = pl.pallas_call(
        kern,
        grid=(per_slot,),
        in_specs=in_specs,
        out_specs=out_specs,
        out_shape=out_shape,
        scratch_shapes=[pltpu.VMEM((SCAN_SLOTS, 2, N_PAIRS, PAIR, PAIR), F32)],
        compiler_params=_params(0, 1),
        name="wkv_scan",
    )(*(args * SCAN_SLOTS))
    st = jnp.concatenate([outs[3 * sl + 2][:n_c] for sl, n_c in enumerate(n_ctx_slot) if n_c], 0)
    return outs[0::3], outs[1::3], st


def _dup_halves(x, lo):
    xr = pltpu.roll(x, HEAD_DIM, 1)
    return jnp.where(lo, x, xr), jnp.where(lo, xr, x)


def _stack_heads(qp, lo):
    zero = jnp.zeros_like(qp)
    return jnp.concatenate([jnp.where(lo, qp, zero), jnp.where(lo, zero, qp)], 0)


def _sink_row(sink_ref, p, rows):
    c = lax.broadcasted_iota(jnp.int32, (1, 2 * rows), 1)
    return jnp.where(c < rows, sink_ref[2 * p:2 * p + 1, 0:1], sink_ref[2 * p + 1:2 * p + 2, 0:1])


def _softmax_pv(s_t, sk, v_t):
    m = jnp.maximum(jnp.max(s_t, 0, keepdims=True), sk)
    e = jnp.exp(s_t - m)
    den = jnp.sum(e, 0, keepdims=True) + jnp.exp(sk - m)
    return jnp.dot(v_t, e.astype(BF16), preferred_element_type=F32) / den


def _unstack_heads_t(o_t, rows):
    top = lax.broadcasted_iota(jnp.int32, (PAIR, rows), 0) < HEAD_DIM
    return jnp.where(top, o_t[:, :rows], o_t[:, rows:]).T


def _attn_ctx_kernel(q_ref, kv_ref, sink_ref, o_ref):
    rows = q_ref.shape[0]
    lo = lax.broadcasted_iota(jnp.int32, (rows, PAIR), 1) < HEAD_DIM
    k_dup = [k.astype(BF16) for k in _dup_halves(kv_ref[:, :ATT_KV], lo)]
    v_t = [v.T.astype(BF16) for v in _dup_halves(kv_ref[:, ATT_KV:], lo)]
    scale = HEAD_DIM ** -0.5
    pairs = range(N_HEADS // 2)
    kv_head = [(2 * p) // (N_HEADS // KV_HEADS) for p in pairs]
    qs = [_stack_heads(q_ref[:, PAIR * p:PAIR * (p + 1)] * scale, lo).astype(BF16) for p in pairs]
    s_t = [lax.dot_general(k_dup[kv_head[p]], qs[p], _NT, preferred_element_type=F32)
           for p in pairs]
    o_t = [_softmax_pv(s_t[p], _sink_row(sink_ref, p, rows), v_t[kv_head[p]]) for p in pairs]
    for p in pairs:
        o_ref[:, PAIR * p:PAIR * (p + 1)] = _unstack_heads_t(o_t[p], rows)


def _attn_ctx(q, kv, sink_rows, layer, n_seq, seq_len):
    return pl.pallas_call(
        _attn_ctx_kernel,
        grid=(n_seq,),
        in_specs=[pl.BlockSpec((seq_len, ATT_Q), lambda b: (b, 0)),
                  pl.BlockSpec((seq_len, 2 * ATT_KV), lambda b: (b, 0)),
                  _at_layer((N_HEADS, 128), layer)],
        out_specs=pl.BlockSpec((seq_len, ATT_Q), lambda b: (b, 0)),
        out_shape=jax.ShapeDtypeStruct((n_seq * seq_len, ATT_Q), F32),
        compiler_params=_params(1),
        name="attn_ctx",
    )(q, kv, sink_rows)


def _rope(x, cos, sgn_sin, first16):
    sw = jnp.where(first16, pltpu.roll(x, PAIR - 16, 1), pltpu.roll(x, 16, 1))
    return x * cos + sw * sgn_sin


def _attn_lat_kernel(q_ref, kv_ref, ck_ref, cv_ref, cos_ref, sin_ref, sink_ref, o_ref,
                     kd_scr, vt_scr, ckd_scr, cvt_scr, *, seq_len):
    i = pl.program_id(1)
    qb = q_ref.shape[0]
    n_blk = seq_len // qb

    @pl.when(i == 0)
    def _():
        lo_s = lax.broadcasted_iota(jnp.int32, (seq_len, PAIR), 1) < HEAD_DIM
        f16_s = (lax.broadcasted_iota(jnp.int32, (seq_len, PAIR), 1) % 32) < 16
        k_r = _rope(kv_ref[:, :ATT_KV], cos_ref[...], sin_ref[...], f16_s)
        k_dup = _dup_halves(k_r, lo_s)
        v_dup = _dup_halves(kv_ref[:, ATT_KV:], lo_s)
        lo_c = lax.broadcasted_iota(jnp.int32, ck_ref.shape[1:], 1) < HEAD_DIM
        ck_dup = _dup_halves(ck_ref[0], lo_c)
        cv_dup = _dup_halves(cv_ref[0], lo_c)
        for g in range(KV_HEADS):
            for b in range(n_blk):
                kd_scr[g, b] = k_dup[g][b * qb:(b + 1) * qb].astype(BF16)
                vt_scr[g, b] = v_dup[g][b * qb:(b + 1) * qb].T.astype(BF16)
            ckd_scr[g] = ck_dup[g].astype(BF16)
            cvt_scr[g] = cv_dup[g].T.astype(BF16)

    lo = lax.broadcasted_iota(jnp.int32, (qb, PAIR), 1) < HEAD_DIM
    f16 = (lax.broadcasted_iota(jnp.int32, (qb, PAIR), 1) % 32) < 16
    q0 = pl.multiple_of(i * qb, qb)
    cos = cos_ref[pl.ds(q0, qb), :]
    sin = sin_ref[pl.ds(q0, qb), :]
    qpos = i * qb + lax.broadcasted_iota(jnp.int32, (qb, 2 * qb), 1) % qb
    krow = lax.broadcasted_iota(jnp.int32, (qb, 2 * qb), 0)
    scale = HEAD_DIM ** -0.5

    pairs = range(N_HEADS // 2)
    kv_head = [(2 * p) // (N_HEADS // KV_HEADS) for p in pairs]
    win = []
    for jj in range(3):
        j = i - 1 + jj
        jc = jnp.clip(j, 0, n_blk - 1)
        kpos = jnp.where(j == jc, jc * qb, -4 * seq_len) + krow
        win.append((jc, jnp.abs(qpos - kpos) <= WINDOW))

    qs = [_stack_heads(_rope(q_ref[:, PAIR * p:PAIR * (p + 1)], cos, sin, f16) * scale,
                       lo).astype(BF16) for p in pairs]
    s_t = []
    for p in pairs:
        g = kv_head[p]
        nt = functools.partial(lax.dot_general, dimension_numbers=_NT, preferred_element_type=F32)
        parts = [nt(ckd_scr[g], qs[p])]
        for jc, ok in win:
            parts.append(jnp.where(ok, nt(kd_scr[g, jc], qs[p]), NEG_INF))
        s_t.append(jnp.concatenate(parts, 0))
    o_t = []
    for p in pairs:
        g = kv_head[p]
        v_t = jnp.concatenate([cvt_scr[g]] + [vt_scr[g, jc] for jc, _ in win], 1)
        o_t.append(_softmax_pv(s_t[p], _sink_row(sink_ref, p, qb), v_t))
    for p in pairs:
        o_ref[:, PAIR * p:PAIR * (p + 1)] = _unstack_heads_t(o_t[p], qb)


def _attn_lat(q, kv, ck, cv, layer, cos_t, sin_t, sink_rows, row0, n_seq, seq_len):
    qb = WINDOW
    n_blk = seq_len // qb
    blk0 = row0 // qb
    seq0 = row0 // seq_len
    n_past = ck.shape[2]
    kern = functools.partial(_attn_lat_kernel, seq_len=seq_len)
    return pl.pallas_call(
        kern,
        grid=(n_seq, n_blk),
        in_specs=[pl.BlockSpec((qb, ATT_Q), lambda b, i: (blk0 + b * n_blk + i, 0)),
                  pl.BlockSpec((seq_len, 2 * ATT_KV), lambda b, i: (seq0 + b, 0)),
                  pl.BlockSpec((1, None, n_past, ATT_KV), lambda b, i: (b, layer, 0, 0)),
                  pl.BlockSpec((1, None, n_past, ATT_KV), lambda b, i: (b, layer, 0, 0)),
                  pl.BlockSpec((seq_len, PAIR), lambda b, i: (0, 0)),
                  pl.BlockSpec((seq_len, PAIR), lambda b, i: (0, 0)),
                  _at_layer((N_HEADS, 128), layer)],
        out_specs=pl.BlockSpec((qb, ATT_Q), lambda b, i: (b * n_blk + i, 0)),
        out_shape=jax.ShapeDtypeStruct((n_seq * seq_len, ATT_Q), F32),
        scratch_shapes=[pltpu.VMEM((KV_HEADS, n_blk, qb, PAIR), BF16),
                        pltpu.VMEM((KV_HEADS, n_blk, PAIR, qb), BF16),
                        pltpu.VMEM((KV_HEADS, n_past, PAIR), BF16),
                        pltpu.VMEM((KV_HEADS, PAIR, n_past), BF16)],
        compiler_params=_params(1, 1),
        name="attn_lat",
    )(q, kv, ck, cv, cos_t, sin_t, sink_rows)


def _postmix_kernel(x_ref, octx_ref, olat_ref, *rest, with_router, n_ctx_tiles, per_slot):
    y_fwd = rest[:SCAN_SLOTS]
    y_bwd = rest[SCAN_SLOTS:2 * SCAN_SLOTS]
    (bonus_ref, g_ref, mod_ref, gpost_ref, gpre_ref, lng_ref, lnb_ref, wo_ref, bd_ref, router_ref,
     x1_ref, h_ref, gates_ref) = rest[2 * SCAN_SLOTS:]
    m = mod_ref[0]
    bd = bd_ref[...]
    tile = lax.broadcasted_iota(jnp.int32, octx_ref.shape, 0) * 0 + pl.program_id(0)
    o_att = jnp.where(tile < n_ctx_tiles, octx_ref[...], olat_ref[...])
    slot = (lax.broadcasted_iota(jnp.int32, y_fwd[0].shape, 0) * 0 + pl.program_id(0)) // per_slot
    y = y_fwd[0][...] + y_bwd[0][...]
    for sl in range(1, SCAN_SLOTS):
        y = jnp.where(slot == sl, y_fwd[sl][...] + y_bwd[sl][...], y)
    inv_n = 1.0 / R_HEAD
    mu = _head_sum(y, bd) * inv_n
    yc = y - mu
    var = _head_sum(yc * yc, bd) * inv_n
    yn = yc * lax.rsqrt(var + GN_EPS)
    o_rw = (yn * lng_ref[...] + lnb_ref[...] + bonus_ref[...]) * g_ref[...]
    o = (jnp.dot(o_att.astype(BF16), wo_ref[:ATT_Q, :], preferred_element_type=F32)
         + jnp.dot(o_rw.astype(BF16), wo_ref[ATT_Q:, :], preferred_element_type=F32))
    x1 = x_ref[...] + m[2:3] * (_rms(o) * gpost_ref[...])
    x1_ref[...] = x1
    h = (_rms(x1) * gpre_ref[...]) * (1.0 + m[4:5]) + m[3:4]
    h_ref[...] = h.astype(BF16)
    if with_router:
        h_hi, h_lo = _split2(h)
        r_hi, r_lo = _split2(router_ref[...])
        d = functools.partial(jnp.dot, preferred_element_type=F32)
        logits = d(h_hi, r_hi) + (d(h_hi, r_lo) + d(h_lo, r_hi))
        lane = lax.broadcasted_iota(jnp.int32, logits.shape, 1)
        valid = lane < N_EXPERTS
        logits = jnp.where(valid, logits, NEG_INF)
        e = jnp.exp(logits - jnp.max(logits, -1, keepdims=True))
        probs = e / jnp.sum(e, -1, keepdims=True)
        lane_f = lane.astype(F32)
        p1 = jnp.max(probs, -1, keepdims=True)
        i1 = jnp.min(jnp.where(probs == p1, lane_f, 1e9), -1, keepdims=True)
        rest = jnp.where(lane_f == i1, -1.0, probs)
        p2 = jnp.max(rest, -1, keepdims=True)
        i2 = jnp.min(jnp.where(rest == p2, lane_f, 1e9), -1, keepdims=True)
        sel = (lane_f == i1) | (lane_f == i2)
        gates_ref[...] = jnp.where(sel, probs / (p1 + p2), 0.0)
    else:
        gates_ref[...] = jnp.ones(gates_ref.shape, F32)


def _postmix(x, o_ctx, o_lat, y0, y1, bonus, g, mods, p, w_o, layer, consts, mod_map,
             router_pad, router_layer, n_ctx_tiles):
    nt = x.shape[0]
    row = lambda i: (i, 0)
    c2 = lambda i: (0, 0)
    with_router = router_layer is not None
    per_slot = nt // TILE // SCAN_SLOTS
    kern = functools.partial(_postmix_kernel, with_router=with_router, n_ctx_tiles=n_ctx_tiles,
                             per_slot=per_slot)
    n_lat_tiles = nt // TILE - n_ctx_tiles
    y_specs = [pl.BlockSpec((TILE, RW),
                            lambda i, sl=sl: (jnp.clip(i - sl * per_slot, 0, per_slot - 1), 0))
               for sl in range(SCAN_SLOTS)]
    return pl.pallas_call(
        kern,
        grid=(nt // TILE,),
        in_specs=[pl.BlockSpec((TILE, D_MODEL), row),
                  pl.BlockSpec((TILE, ATT_Q), lambda i: (jnp.minimum(i, n_ctx_tiles - 1), 0)),
                  pl.BlockSpec((TILE, ATT_Q),
                               lambda i: (jnp.clip(i - n_ctx_tiles, 0, n_lat_tiles - 1), 0)),
                  *y_specs, *y_specs,
                  pl.BlockSpec((TILE, RW), row),
                  pl.BlockSpec((TILE, RW), row),
                  pl.BlockSpec((None, 1, 6, D_MODEL), lambda i: (layer, mod_map(i), 0, 0)),
                  _at_layer((1, D_MODEL), layer),
                  _at_layer((1, D_MODEL), layer),
                  _at_layer((1, RW), layer),
                  _at_layer((1, RW), layer),
                  _at_layer((ATT_Q + RW, D_MODEL), layer),
                  pl.BlockSpec((RW, RW), c2),
                  _at_layer((D_MODEL, 128), router_layer or 0)],
        out_specs=[pl.BlockSpec((TILE, D_MODEL), row),
                   pl.BlockSpec((TILE, D_MODEL), row),
                   pl.BlockSpec((TILE, 128), row)],
        out_shape=[jax.ShapeDtypeStruct((nt, D_MODEL), F32),
                   jax.ShapeDtypeStruct((nt, D_MODEL), BF16),
                   jax.ShapeDtypeStruct((nt, 128), F32)],
        compiler_params=_params(1),
        name="postmix",
    )(x, o_ctx, o_lat, *y0, *y1, bonus, g, mods, p["g_post_mix"], p["g_pre_ffn"],
      p["ln_g"], p["ln_b"], w_o, consts["bd"], router_pad)


def _ffn_kernel(h_ref, x1_ref, gates_ref, mod_ref, gpost_ref, w1_ref, w3_ref, w2_ref, o_ref,
                *, n_blocks, experts_per_block):
    blk = pl.program_id(1)

    @pl.when(blk == 0)
    def _():
        o_ref[...] = jnp.zeros_like(o_ref)

    h = h_ref[...]
    lane = lax.broadcasted_iota(jnp.int32, gates_ref.shape, 1)
    gate = [jnp.sum(jnp.where(lane == blk * experts_per_block + j, gates_ref[...], 0.0), -1,
                    keepdims=True) for j in range(experts_per_block)]
    width = w1_ref.shape[-1]
    per_expert = width // experts_per_block
    out = None
    for lo in range(0, width, FF_SUB):
        hi = min(lo + FF_SUB, width)
        a = jnp.dot(h, w1_ref[0, :, lo:hi], preferred_element_type=F32)
        b = jnp.dot(h, w3_ref[0, :, lo:hi], preferred_element_type=F32)
        col = lo + lax.broadcasted_iota(jnp.int32, (1, hi - lo), 1)
        g_col = gate[-1]
        for j in range(experts_per_block - 2, -1, -1):
            g_col = jnp.where(col < (j + 1) * per_expert, gate[j], g_col)
        t = ((a * _sigmoid(a)) * b) * g_col
        part = jnp.dot(t.astype(BF16), w2_ref[0, lo:hi, :], preferred_element_type=F32)
        out = part if out is None else out + part
    o_ref[...] += out

    @pl.when(blk == n_blocks - 1)
    def _():
        m = mod_ref[0]
        o_ref[...] = x1_ref[...] + m[5:6] * (_rms(o_ref[...]) * gpost_ref[...])


def _ffn(h, x1, gates, mods, g_post, w1, w3, w2, layer, w_layer, experts_per_block, mod_map_ffn):
    nt = h.shape[0]
    _, n_blocks, _, width = w1.shape
    row = lambda i, b: (i, 0)
    kern = functools.partial(_ffn_kernel, n_blocks=n_blocks, experts_per_block=experts_per_block)
    return pl.pallas_call(
        kern,
        grid=(nt // FFN_TILE, n_blocks),
        in_specs=[pl.BlockSpec((FFN_TILE, D_MODEL), row),
                  pl.BlockSpec((FFN_TILE, D_MODEL), row),
                  pl.BlockSpec((FFN_TILE, 128), row),
                  pl.BlockSpec((None, 1, 6, D_MODEL), lambda i, b: (layer, mod_map_ffn(i), 0, 0)),
                  _at_layer((1, D_MODEL), layer),
                  pl.BlockSpec((None, 1, D_MODEL, width), lambda i, b: (w_layer, b, 0, 0)),
                  pl.BlockSpec((None, 1, D_MODEL, width), lambda i, b: (w_layer, b, 0, 0)),
                  pl.BlockSpec((None, 1, width, D_MODEL), lambda i, b: (w_layer, b, 0, 0))],
        out_specs=pl.BlockSpec((FFN_TILE, D_MODEL), row),
        out_shape=jax.ShapeDtypeStruct((nt, D_MODEL), F32),
        compiler_params=_params(1, 1, FFN_VMEM_LIMIT),
        name="ffn",
    )(h, x1, gates, mods, g_post, w1, w3, w2)


def _rope_tables(seq_len):
    half = HEAD_DIM // 2
    nf = half // 2
    inv = ROPE_THETA ** (-jnp.arange(nf, dtype=F32) / nf)
    t = jnp.arange(seq_len)
    row = (t // GRID_W).astype(F32)
    col = (t % GRID_W).astype(F32)
    lane = jnp.arange(PAIR)
    pos = jnp.where(((lane % HEAD_DIM) // half == 0)[None, :], row[:, None], col[:, None])
    ang = pos * inv[lane % nf][None, :]
    first = ((lane % half) < nf)[None, :]
    return jnp.cos(ang), jnp.where(first, -jnp.sin(ang), jnp.sin(ang))


def _constants():
    r = jnp.arange(RW)
    bd = (r[:, None] // R_HEAD == r[None, :] // R_HEAD).astype(BF16)
    t = jnp.arange(TILE)
    same = t[:, None] // CHUNK == t[None, :] // CHUNK
    tri = jnp.stack([same & (t[None, :] <= t[:, None]), same & (t[None, :] >= t[:, None])])
    return {"bd": bd, "tri": tri.astype(BF16)}


def _pad_rows(w2):
    z = jnp.zeros_like(w2[:, 0])
    return jnp.stack([jnp.concatenate([w2[:, 0], z], 1), jnp.concatenate([z, w2[:, 1]], 1)],
                     1).astype(BF16)


def kernel(x_prompt, x_sample, cache_k, cache_v, state_rwkv, c, c_ctx, w_mod, b_mod, g_pre_mix, g_post_mix, g_pre_ffn, g_post_ffn, w_in, mu_shift, w_o, attn_sink, rw_w0, rw_w1, rw_w2, rw_a0, rw_a1, rw_a2, rw_g1, rw_g2, rw_k_k, rw_k_a, rw_r_k, rw_ln_g, rw_ln_b, ffn_w1, ffn_w3, ffn_w2, moe_router, moe_w1, moe_w3, moe_w2):
    n_ctx, s_ctx, _ = x_prompt.shape
    n_lat, s_lat, _ = x_sample.shape
    n_past = cache_k.shape[2]
    assert s_ctx == TILE and s_lat % TILE == 0 and (n_ctx * s_ctx) % s_lat == 0
    nt_ctx = n_ctx * s_ctx
    n_ctx_tiles = nt_ctx // TILE
    tiles_per_lat = s_lat // TILE
    mod_map = functools.partial(_mod_row, n_ctx_tiles=n_ctx_tiles, tiles_per_lat=tiles_per_lat)
    ffn_per = FFN_TILE // TILE
    mod_map_ffn = lambda i: mod_map(i * ffn_per)

    x = jnp.concatenate([x_prompt.reshape(nt_ctx, D_MODEL), x_sample.reshape(n_lat * s_lat, D_MODEL)], 0)
    n_cond = 8
    cond = jnp.concatenate([c_ctx[None, :], c, jnp.zeros((n_cond - 1 - n_lat, D_MODEL), F32)], 0)
    mods = _modulation(cond, w_mod, b_mod).reshape(DEPTH, n_cond, 6, D_MODEL)

    consts = _constants()
    cos_t, sin_t = _rope_tables(s_lat)
    w_cat = jnp.concatenate(
        [w_in, rw_w1[:, 0], rw_w1[:, 1], rw_a1[:, 0], rw_a1[:, 1], rw_g1], 2).astype(BF16)
    w_o_b = w_o.astype(BF16)
    dense_w = [w.astype(BF16)[:, None] for w in (ffn_w1, ffn_w3, ffn_w2)]
    n_moe = moe_w1.shape[0]
    side_by_side = lambda w: (w.astype(BF16).reshape(n_moe, N_EXPERTS // 2, 2, D_MODEL, D_FF_EXPERT)
                              .transpose(0, 1, 3, 2, 4)
                              .reshape(n_moe, N_EXPERTS // 2, D_MODEL, 2 * D_FF_EXPERT))
    moe_w = [side_by_side(moe_w1), side_by_side(moe_w3),
             moe_w2.astype(BF16).reshape(n_moe, N_EXPERTS // 2, 2 * D_FF_EXPERT, D_MODEL)]
    router_pad = jnp.pad(moe_router, ((0, 0), (0, 0), (0, 128 - N_EXPERTS)))
    sink_rows = jnp.broadcast_to(attn_sink[:, :, None], (DEPTH, N_HEADS, 128))
    ck = cache_k.reshape(n_lat, DEPTH, n_past, ATT_KV)
    cv = cache_v.reshape(n_lat, DEPTH, n_past, ATT_KV)

    vecs = lambda a: a.reshape(DEPTH, 1, -1)
    p = {"mu": vecs(mu_shift), "w0": rw_w0, "a0": rw_a0, "w2": _pad_rows(rw_w2),
         "a2": _pad_rows(rw_a2), "g2": rw_g2.astype(BF16), "k_k": vecs(rw_k_k),
         "k_a": vecs(rw_k_a), "r_k": vecs(rw_r_k), "ln_g": vecs(rw_ln_g), "ln_b": vecs(rw_ln_b),
         "g_post_mix": vecs(g_post_mix), "g_pre_ffn": vecs(g_pre_ffn)}
    g_pre_mix_v = vecs(g_pre_mix)
    g_post_ffn_v = vecs(g_post_ffn)

    new_k, new_v, new_s = [], [], []
    for l in range(DEPTH):
        q, kv, v_b, al, be, ka, rh, wl, bonus, g = _inprep(
            x, mods, g_pre_mix_v, w_cat, p, l, consts, mod_map, n_ctx_tiles, tiles_per_lat)
        y0, y1, s_ctx_new = _scan(al, be, ka, rh, v_b, wl, state_rwkv[:, l],
                                  n_ctx_tiles, tiles_per_lat)
        o_ctx = _attn_ctx(q, kv, sink_rows, l, n_ctx, s_ctx)
        o_lat = _attn_lat(q, kv, ck, cv, l, cos_t, sin_t, sink_rows, nt_ctx, n_lat, s_lat)

        i = l // 2
        moe = l % 2 == 1
        x1, h, gates = _postmix(x, o_ctx, o_lat, y0, y1, bonus, g, mods, p, w_o_b, l, consts,
                                mod_map, router_pad, i if moe else None, n_ctx_tiles)
        w1, w3, w2 = moe_w if moe else dense_w
        x = _ffn(h, x1, gates, mods, g_post_ffn_v, w1, w3, w2, l, i, 2 if moe else 1,
                 mod_map_ffn)

        new_k.append(kv[:nt_ctx, :ATT_KV].reshape(n_ctx, s_ctx, KV_HEADS, HEAD_DIM))
        new_v.append(kv[:nt_ctx, ATT_KV:].reshape(n_ctx, s_ctx, KV_HEADS, HEAD_DIM))
        new_s.append(s_ctx_new)

    y_prompt = x[:nt_ctx].reshape(n_ctx, s_ctx, D_MODEL)
    y_sample = x[nt_ctx:].reshape(n_lat, s_lat, D_MODEL)
    return (y_prompt, y_sample, jnp.stack(new_k, 1), jnp.stack(new_v, 1), jnp.stack(new_s, 1))
```

```python
import functools

import jax
import jax.numpy as jnp
from jax import lax
from jax.experimental import pallas as pl
from jax.experimental.pallas import tpu as pltpu

F32 = jnp.float32
BF16 = jnp.bfloat16

D_MODEL = 1024
DEPTH = 4
GRID_W = 64
N_HEADS = 8
KV_HEADS = 2
HEAD_DIM = 64
WINDOW = 128
ROPE_THETA = 10000.0
R_HEADS = 8
R_HEAD = 64
RW = R_HEADS * R_HEAD
LORA_W = 64
LORA_A = 64
LORA_G = 128
LORA_ALL = 2 * LORA_W + 2 * LORA_A + LORA_G
ATT_Q = N_HEADS * HEAD_DIM
ATT_KV = KV_HEADS * HEAD_DIM
D_IN = ATT_Q + 2 * ATT_KV + 3 * RW
D_FF = 2816
N_EXPERTS = 8
D_FF_EXPERT = 1408
NORM_EPS = 1e-6
GN_EPS = 64e-5
NEG_INF = -1e30

TILE = 256
CHUNK = 64
PAIR = 2 * R_HEAD
N_PAIRS = R_HEADS // 2
SCAN_SLOTS = 3
FFN_TILE = 512
FF_SUB = 1024
MOD_COLS = 1536
VMEM_LIMIT = 48 * 1024 * 1024
FFN_VMEM_LIMIT = 58 * 1024 * 1024

_NT = (((1,), (1,)), ((), ()))


def _params(n_parallel, n_arbitrary=0, vmem_limit=VMEM_LIMIT):
    sem = ("parallel",) * n_parallel + ("arbitrary",) * n_arbitrary
    return pltpu.CompilerParams(dimension_semantics=sem, vmem_limit_bytes=vmem_limit)


def _sigmoid(x):
    return 0.5 * jnp.tanh(0.5 * x) + 0.5


def _mm(a, b):
    return jnp.dot(a.astype(BF16), b.astype(BF16), preferred_element_type=F32)


def _mm_nt(a, b):
    return lax.dot_general(a.astype(BF16), b.astype(BF16), _NT, preferred_element_type=F32)


def _split2(x):
    hi = x.astype(BF16)
    return hi, (x - hi.astype(F32)).astype(BF16)


def _head_sum(x, ones_bf16):
    return jnp.dot(x.astype(BF16), ones_bf16, preferred_element_type=F32)


def _mm_ones_left(ones_bf16, x):
    hi, lo = _split2(x)
    d = functools.partial(jnp.dot, preferred_element_type=F32)
    return d(ones_bf16, hi) + d(ones_bf16, lo)


def _rms(x):
    return x * lax.rsqrt(jnp.mean(x * x, -1, keepdims=True) + NORM_EPS)


def _at_layer(shape, layer):
    zeros = (0,) * len(shape)
    return pl.BlockSpec((None,) + tuple(shape), lambda *_: (layer,) + zeros)


def _mod_row(i, n_ctx_tiles, tiles_per_lat):
    return jnp.where(i < n_ctx_tiles, 0, 1 + (i - n_ctx_tiles) // tiles_per_lat)


def _mod_kernel(cond_ref, w_ref, b_ref, o_ref):
    c = cond_ref[...]
    s = c * _sigmoid(c)
    o_ref[0] = _mm(s, w_ref[0]) + b_ref[0]


def _modulation(cond, w_mod, b_mod):
    n_cond = cond.shape[0]
    n_col = 6 * D_MODEL // MOD_COLS
    return pl.pallas_call(
        _mod_kernel,
        grid=(DEPTH, n_col),
        in_specs=[pl.BlockSpec((n_cond, D_MODEL), lambda l, j: (0, 0)),
                  pl.BlockSpec((1, D_MODEL, MOD_COLS), lambda l, j: (l, 0, j)),
                  pl.BlockSpec((1, 1, MOD_COLS), lambda l, j: (l, 0, j))],
        out_specs=pl.BlockSpec((1, n_cond, MOD_COLS), lambda l, j: (l, 0, j)),
        out_shape=jax.ShapeDtypeStruct((DEPTH, n_cond, 6 * D_MODEL), F32),
        compiler_params=_params(2),
        name="modulation",
    )(cond, w_mod, b_mod.reshape(DEPTH, 1, 6 * D_MODEL))


HALO = 8


def _inprep_kernel(x_ref, xp_ref, xn_ref, mod_ref, g_ref, w_ref, mu_ref, w0_ref, w2_ref, a0_ref,
                   a2_ref, g2_ref, kk_ref, ka_ref, rk_ref, bd_ref, tri_ref,
                   q_out, kv_out, v_out, al_out, be_out, ka_out, rh_out, wl_out, bonus_out, g_out,
                   *, n_ctx_tiles, tiles_per_lat):
    i = pl.program_id(0)
    j = (i - n_ctx_tiles) % tiles_per_lat
    is_ctx = i < n_ctx_tiles
    first = is_ctx | (j == 0)
    last = is_ctx | (j == tiles_per_lat - 1)

    m = mod_ref[0]
    x_all = jnp.concatenate([xp_ref[...], x_ref[...], xn_ref[...]], 0)
    h = (_rms(x_all) * g_ref[...]) * (1.0 + m[1:2]) + m[0:1]
    main = slice(HALO, HALO + TILE)
    h_main = h[main].astype(BF16)
    n_att = ATT_Q + 2 * ATT_KV
    u_all = jnp.dot(h.astype(BF16), w_ref[:, n_att:], preferred_element_type=F32)
    rkv = slice(0, D_IN - n_att)
    u = u_all[main, rkv]
    lora = u_all[main, D_IN - n_att:]

    row = lax.broadcasted_iota(jnp.int32, (TILE, 1), 0)
    p_row = u_all[HALO - 1:HALO, rkv] * (1.0 - first.astype(F32))
    n_row = u_all[HALO + TILE:HALO + TILE + 1, rkv] * (1.0 - last.astype(F32))
    prev = jnp.where(row == 0, p_row, pltpu.roll(u, 1, 0))
    nxt = jnp.where(row == TILE - 1, n_row, pltpu.roll(u, TILE - 1, 0))
    us = u + mu_ref[...] * (0.5 * (prev + nxt) - u)
    r = us[:, :RW]
    k = us[:, RW:2 * RW]
    v = us[:, 2 * RW:]

    bd = bd_ref[...]
    kk = k * kk_ref[...]
    kk = kk * lax.rsqrt(_head_sum(kk * kk, bd) + 1e-12)

    lw_in = jnp.tanh(lora[:, :2 * LORA_W])
    la_in = lora[:, 2 * LORA_W:2 * LORA_W + 2 * LORA_A]
    kd_sum = jnp.zeros_like(k)
    for d in range(2):
        w_pre = w0_ref[d:d + 1, :] + _mm(lw_in, w2_ref[d])
        lw = -jnp.exp(-0.5) * _sigmoid(w_pre)
        a = _sigmoid(a0_ref[d:d + 1, :] + _mm(la_in, a2_ref[d]))
        kd = k * (1.0 + (a - 1.0) * ka_ref[...])
        kd_sum = kd_sum + kd
        cw = _mm_ones_left(tri_ref[d], lw)
        e_neg = jnp.exp(-cw)
        al_out[d] = (kk * jnp.exp(cw - lw)).astype(BF16)
        be_out[d] = (kk * a * e_neg).astype(BF16)
        ka_out[d] = (kd * e_neg).astype(BF16)
        rh_out[d] = (r * jnp.exp(cw)).astype(BF16)
        for c in range(TILE // CHUNK):
            end = c * CHUNK + (CHUNK - 1 if d == 0 else 0)
            wl_out[0, 4 * d + c:4 * d + c + 1, :] = jnp.exp(cw[end:end + 1, :])
    v_out[...] = v.astype(BF16)
    bonus_out[...] = _head_sum(r * kd_sum * rk_ref[...], bd) * v
    g_out[...] = _mm(_sigmoid(lora[:, 2 * LORA_W + 2 * LORA_A:]), g2_ref[...])
    u_att = jnp.dot(h_main, w_ref[:, :n_att], preferred_element_type=F32)
    q_out[...] = u_att[:, :ATT_Q]
    kv_out[...] = u_att[:, ATT_Q:]


def _inprep(x, mods, g, w_cat, p, layer, consts, mod_map, n_ctx_tiles, tiles_per_lat):
    nt = x.shape[0]
    n_tiles = nt // TILE
    row = lambda i: (i, 0)
    drow = lambda i: (0, i, 0)
    hb = TILE // HALO
    kern = functools.partial(_inprep_kernel, n_ctx_tiles=n_ctx_tiles, tiles_per_lat=tiles_per_lat)
    dshape = jax.ShapeDtypeStruct((2, nt, RW), BF16)
    return pl.pallas_call(
        kern,
        grid=(n_tiles,),
        in_specs=[pl.BlockSpec((TILE, D_MODEL), row),
                  pl.BlockSpec((HALO, D_MODEL), lambda i: (jnp.maximum(i * hb - 1, 0), 0)),
                  pl.BlockSpec((HALO, D_MODEL),
                               lambda i: (jnp.minimum((i + 1) * hb, nt // HALO - 1), 0)),
                  pl.BlockSpec((None, 1, 6, D_MODEL), lambda i: (layer, mod_map(i), 0, 0)),
                  _at_layer((1, D_MODEL), layer),
                  _at_layer((D_MODEL, D_IN + LORA_ALL), layer),
                  _at_layer((1, 3 * RW), layer),
                  _at_layer((2, RW), layer),
                  _at_layer((2, 2 * LORA_W, RW), layer),
                  _at_layer((2, RW), layer),
                  _at_layer((2, 2 * LORA_A, RW), layer),
                  _at_layer((LORA_G, RW), layer),
                  _at_layer((1, RW), layer),
                  _at_layer((1, RW), layer),
                  _at_layer((1, RW), layer),
                  pl.BlockSpec((RW, RW), lambda i: (0, 0)),
                  pl.BlockSpec((2, TILE, TILE), lambda i: (0, 0, 0))],
        out_specs=[pl.BlockSpec((TILE, ATT_Q), row),
                   pl.BlockSpec((TILE, 2 * ATT_KV), row),
                   pl.BlockSpec((TILE, RW), row),
                   pl.BlockSpec((2, TILE, RW), drow),
                   pl.BlockSpec((2, TILE, RW), drow),
                   pl.BlockSpec((2, TILE, RW), drow),
                   pl.BlockSpec((2, TILE, RW), drow),
                   pl.BlockSpec((1, 8, RW), lambda i: (i, 0, 0)),
                   pl.BlockSpec((TILE, RW), row),
                   pl.BlockSpec((TILE, RW), row)],
        out_shape=[jax.ShapeDtypeStruct((nt, ATT_Q), F32),
                   jax.ShapeDtypeStruct((nt, 2 * ATT_KV), F32),
                   jax.ShapeDtypeStruct((nt, RW), BF16), dshape, dshape, dshape, dshape,
                   jax.ShapeDtypeStruct((n_tiles, 8, RW), F32),
                   jax.ShapeDtypeStruct((nt, RW), F32),
                   jax.ShapeDtypeStruct((nt, RW), F32)],
        compiler_params=_params(1),
        name="inproj_prep",
    )(x, x, x, mods, g, w_cat, p["mu"], p["w0"], p["w2"], p["a0"], p["a2"], p["g2"],
      p["k_k"], p["k_a"], p["r_k"], consts["bd"], consts["tri"])


def _scan_kernel(*refs, n_ctx_tiles, tiles_per_lat, per_slot):
    n_in = 13
    slots = range(SCAN_SLOTS)
    ins = [refs[n_in * sl:n_in * (sl + 1)] for sl in slots]
    outs = [refs[SCAN_SLOTS * n_in + 3 * sl:SCAN_SLOTS * n_in + 3 * (sl + 1)] for sl in slots]
    s_scr = refs[-1]
    for sl in slots:
        tile = sl * per_slot + pl.program_id(0)
        is_ctx = tile < n_ctx_tiles
        seq_start = (tile - n_ctx_tiles) % tiles_per_lat == 0

        @pl.when(is_ctx)
        def _():
            s_scr[sl] = jnp.zeros(s_scr.shape[1:], F32)

        @pl.when(jnp.logical_not(is_ctx) & seq_start)
        def _():
            s_scr[sl] = ins[sl][12][...]

    lo = lax.broadcasted_iota(jnp.int32, (CHUNK, PAIR), 1) < R_HEAD
    rt = lax.broadcasted_iota(jnp.int32, (CHUNK, PAIR), 0)
    ct = lax.broadcasted_iota(jnp.int32, (CHUNK, PAIR), 1) % CHUNK
    before = [ct < rt, ct > rt]
    upto = [ct <= rt, ct >= rt]
    same_head = (lax.broadcasted_iota(jnp.int32, (PAIR, PAIR), 0) // R_HEAD
                 == lax.broadcasted_iota(jnp.int32, (PAIR, PAIR), 1) // R_HEAD)
    n_sq = CHUNK.bit_length() - 2
    n_chunks = TILE // CHUNK

    def diag(x):
        x = x.astype(BF16)
        zero = jnp.zeros_like(x)
        return jnp.concatenate([jnp.where(lo, x, zero), jnp.where(lo, zero, x)], 0)

    def chunk(c, carry):
        chains = []
        for sl in slots:
            for d in range(2):
                al, be, ka, rh, v, wl = ins[sl][6 * d:6 * (d + 1)]
                y_ref = outs[sl][d]
                ce = c if d == 0 else n_chunks - 1 - c
                rows = pl.ds(pl.multiple_of(ce * CHUNK, CHUNK), CHUNK)
                wl_all = wl[pl.ds(n_chunks * d + ce, 1), :]
                for p in range(N_PAIRS):
                    lanes = slice(PAIR * p, PAIR * (p + 1))
                    chains.append(dict(
                        sl=sl, d=d, p=p, rows=rows, lanes=lanes, y_ref=y_ref,
                        wl=wl_all[:, lanes], a=al[rows, lanes], b=be[rows, lanes],
                        k=ka[rows, lanes], r=rh[rows, lanes], v=v[rows, lanes],
                        s=s_scr[sl, d, p]))

        zero = jnp.zeros((CHUNK, PAIR), F32)
        for ch in chains:
            ar = jnp.concatenate([ch["a"], ch["r"]], 0)
            ch["v_d"] = diag(ch["v"])
            bks = jnp.concatenate([diag(ch["b"]), diag(ch["k"]), ch["s"].astype(BF16)], 0)
            ch["gram"] = lax.dot_general(ar, bks, _NT, preferred_element_type=F32)
        for ch in chains:
            g, d = ch["gram"], ch["d"]
            ch["m_ab"] = jnp.where(before[d], g[:CHUNK, :PAIR], zero)
            ch["m_rb"] = jnp.where(upto[d], g[CHUNK:, :PAIR], zero)
            m_ak = jnp.where(before[d], g[:CHUNK, PAIR:2 * PAIR], zero)
            m_rk = jnp.where(upto[d], g[CHUNK:, PAIR:2 * PAIR], zero)
            ch["mv"] = _mm(jnp.concatenate([m_ak, m_rk], 0), ch["v_d"])

        for ch in chains:
            ch["x"] = ch["gram"][:CHUNK, 2 * PAIR:] + ch["mv"][:CHUNK]
            ch["pw"] = ch["m_ab"]
        for lvl in range(n_sq + 1):
            for ch in chains:
                if lvl < n_sq:
                    px = _mm(ch["pw"], jnp.concatenate([diag(ch["pw"]), diag(ch["x"])], 1))
                    ch["pw"] = px[:, :PAIR]
                    step = px[:, PAIR:]
                else:
                    step = _mm(ch["pw"], diag(ch["x"]))
                ch["x"] = ch["x"] - step if lvl == 0 else ch["x"] + step

        for ch in chains:
            z = -ch["x"]
            ch["y"] = ch["gram"][CHUNK:, 2 * PAIR:] + ch["mv"][CHUNK:] + _mm(ch["m_rb"], diag(z))
            ch["zv_t"] = jnp.concatenate([z, ch["v"].astype(F32)], 0).T
        for ch in chains:
            bk = jnp.concatenate([ch["b"], ch["k"]], 0)
            upd = jnp.where(same_head, _mm(ch["zv_t"], bk), 0.0)
            ch["s_new"] = (ch["s"] + upd) * ch["wl"]
        for ch in chains:
            ch["y_ref"][ch["rows"], ch["lanes"]] = ch["y"]
            s_scr[ch["sl"], ch["d"], ch["p"]] = ch["s_new"]
        return carry

    lax.fori_loop(0, n_chunks, chunk, 0)
    for sl in slots:
        st_ref = outs[sl][2]
        for d in range(2):
            for p in range(N_PAIRS):
                s = s_scr[sl, d, p]
                st_ref[d, 2 * p] = s[:R_HEAD, :R_HEAD]
                st_ref[d, 2 * p + 1] = s[R_HEAD:, R_HEAD:]


def _expand_state(s0):
    n = s0.shape[0]
    s = s0.reshape(n, 2, N_PAIRS, 2, R_HEAD, R_HEAD)
    z = jnp.zeros_like(s[:, :, :, 0])
    top = jnp.concatenate([s[:, :, :, 0], z], -1)
    bot = jnp.concatenate([z, s[:, :, :, 1]], -1)
    return jnp.concatenate([top, bot], -2)


def _scan(al, be, ka, rh, v, wl, s0_lat, n_ctx_tiles, tiles_per_lat):
    nt = v.shape[0]
    n_tiles = nt // TILE
    per_slot = n_tiles // SCAN_SLOTS
    n_lat = s0_lat.shape[0]
    assert n_tiles % SCAN_SLOTS == 0
    for t0 in range(per_slot, n_tiles, per_slot):
        assert t0 <= n_ctx_tiles or (t0 - n_ctx_tiles) % tiles_per_lat == 0

    def mirror(t):
        j = (t - n_ctx_tiles) % tiles_per_lat
        return jnp.where(t < n_ctx_tiles, t, t - j + (tiles_per_lat - 1 - j))

    lat_seq = lambda t: jnp.clip((t - n_ctx_tiles) // tiles_per_lat, 0, n_lat - 1)
    state_in = (None, 2, N_PAIRS, PAIR, PAIR)
    state_out = (None, 2, R_HEADS, R_HEAD, R_HEAD)
    in_specs, out_specs, out_shape, n_ctx_slot = [], [], [], []
    for sl in range(SCAN_SLOTS):
        t0 = sl * per_slot
        fwd3 = pl.BlockSpec((None, TILE, RW), lambda i, t0=t0: (0, t0 + i, 0))
        bwd3 = pl.BlockSpec((None, TILE, RW), lambda i, t0=t0: (1, mirror(t0 + i), 0))
        fwd2 = pl.BlockSpec((TILE, RW), lambda i, t0=t0: (t0 + i, 0))
        bwd2 = pl.BlockSpec((TILE, RW), lambda i, t0=t0: (mirror(t0 + i), 0))
        wl_f = pl.BlockSpec((None, 8, RW), lambda i, t0=t0: (t0 + i, 0, 0))
        wl_b = pl.BlockSpec((None, 8, RW), lambda i, t0=t0: (mirror(t0 + i), 0, 0))
        in_specs += [fwd3, fwd3, fwd3, fwd3, fwd2, wl_f, bwd3, bwd3, bwd3, bwd3, bwd2, wl_b,
                     pl.BlockSpec(state_in, lambda i, t0=t0: (lat_seq(t0 + i), 0, 0, 0, 0))]
        n_c = min(max(n_ctx_tiles - t0, 0), per_slot)
        n_park = 1 if n_c < per_slot else 0
        n_ctx_slot.append(n_c)
        out_specs += [pl.BlockSpec((TILE, RW), lambda i: (i, 0)),
                      pl.BlockSpec((TILE, RW), lambda i, t0=t0: (mirror(t0 + i) - t0, 0)),
                      pl.BlockSpec(state_out, lambda i, n_c=n_c: (jnp.minimum(i, n_c), 0, 0, 0, 0))]
        out_shape += [jax.ShapeDtypeStruct((per_slot * TILE, RW), F32),
                      jax.ShapeDtypeStruct((per_slot * TILE, RW), F32),
                      jax.ShapeDtypeStruct((n_c + n_park, 2, R_HEADS, R_HEAD, R_HEAD), F32)]
    kern = functools.partial(_scan_kernel, n_ctx_tiles=n_ctx_tiles, tiles_per_lat=tiles_per_lat,
                             per_slot=per_slot)
    s0 = _expand_state(s0_lat)
    args = (al, be, ka, rh, v, wl, al, be, ka, rh, v, wl, s0)
    outs = pl.pallas_call(
        kern,
        grid=(per_slot,),
        in_specs=in_specs,
        out_specs=out_specs,
        out_shape=out_shape,
        scratch_shapes=[pltpu.VMEM((SCAN_SLOTS, 2, N_PAIRS, PAIR, PAIR), F32)],
        compiler_params=_params(0, 1),
        name="wkv_scan",
    )(*(args * SCAN_SLOTS))
    st = jnp.concatenate([outs[3 * sl + 2][:n_c] for sl, n_c in enumerate(n_ctx_slot) if n_c], 0)
    return outs[0::3], outs[1::3], st


def _dup_halves(x, lo):
    xr = pltpu.roll(x, HEAD_DIM, 1)
    return jnp.where(lo, x, xr), jnp.where(lo, xr, x)


def _stack_heads(qp, lo):
    zero = jnp.zeros_like(qp)
    return jnp.concatenate([jnp.where(lo, qp, zero), jnp.where(lo, zero, qp)], 0)


def _sink_row(sink_ref, p, rows):
    c = lax.broadcasted_iota(jnp.int32, (1, 2 * rows), 1)
    return jnp.where(c < rows, sink_ref[2 * p:2 * p + 1, 0:1], sink_ref[2 * p + 1:2 * p + 2, 0:1])


def _softmax_pv(s_t, sk, v_t):
    m = jnp.maximum(jnp.max(s_t, 0, keepdims=True), sk)
    e = jnp.exp(s_t - m)
    den = jnp.sum(e, 0, keepdims=True) + jnp.exp(sk - m)
    return jnp.dot(v_t, e.astype(BF16), preferred_element_type=F32) / den


def _unstack_heads_t(o_t, rows):
    top = lax.broadcasted_iota(jnp.int32, (PAIR, rows), 0) < HEAD_DIM
    return jnp.where(top, o_t[:, :rows], o_t[:, rows:]).T


def _attn_ctx_kernel(q_ref, kv_ref, sink_ref, o_ref):
    rows = q_ref.shape[0]
    lo = lax.broadcasted_iota(jnp.int32, (rows, PAIR), 1) < HEAD_DIM
    k_dup = [k.astype(BF16) for k in _dup_halves(kv_ref[:, :ATT_KV], lo)]
    v_t = [v.T.astype(BF16) for v in _dup_halves(kv_ref[:, ATT_KV:], lo)]
    scale = HEAD_DIM ** -0.5
    pairs = range(N_HEADS // 2)
    kv_head = [(2 * p) // (N_HEADS // KV_HEADS) for p in pairs]
    qs = [_stack_heads(q_ref[:, PAIR * p:PAIR * (p + 1)] * scale, lo).astype(BF16) for p in pairs]
    s_t = [lax.dot_general(k_dup[kv_head[p]], qs[p], _NT, preferred_element_type=F32)
           for p in pairs]
    o_t = [_softmax_pv(s_t[p], _sink_row(sink_ref, p, rows), v_t[kv_head[p]]) for p in pairs]
    for p in pairs:
        o_ref[:, PAIR * p:PAIR * (p + 1)] = _unstack_heads_t(o_t[p], rows)


def _attn_ctx(q, kv, sink_rows, layer, n_seq, seq_len):
    return pl.pallas_call(
        _attn_ctx_kernel,
        grid=(n_seq,),
        in_specs=[pl.BlockSpec((seq_len, ATT_Q), lambda b: (b, 0)),
                  pl.BlockSpec((seq_len, 2 * ATT_KV), lambda b: (b, 0)),
                  _at_layer((N_HEADS, 128), layer)],
        out_specs=pl.BlockSpec((seq_len, ATT_Q), lambda b: (b, 0)),
        out_shape=jax.ShapeDtypeStruct((n_seq * seq_len, ATT_Q), F32),
        compiler_params=_params(1),
        name="attn_ctx",
    )(q, kv, sink_rows)


def _rope(x, cos, sgn_sin, first16):
    sw = jnp.where(first16, pltpu.roll(x, PAIR - 16, 1), pltpu.roll(x, 16, 1))
    return x * cos + sw * sgn_sin


def _attn_lat_kernel(q_ref, kv_ref, ck_ref, cv_ref, cos_ref, sin_ref, sink_ref, o_ref,
                     kd_scr, vt_scr, ckd_scr, cvt_scr, *, seq_len):
    i = pl.program_id(1)
    qb = q_ref.shape[0]
    n_blk = seq_len // qb

    @pl.when(i == 0)
    def _():
        lo_s = lax.broadcasted_iota(jnp.int32, (seq_len, PAIR), 1) < HEAD_DIM
        f16_s = (lax.broadcasted_iota(jnp.int32, (seq_len, PAIR), 1) % 32) < 16
        k_r = _rope(kv_ref[:, :ATT_KV], cos_ref[...], sin_ref[...], f16_s)
        k_dup = _dup_halves(k_r, lo_s)
        v_dup = _dup_halves(kv_ref[:, ATT_KV:], lo_s)
        lo_c = lax.broadcasted_iota(jnp.int32, ck_ref.shape[1:], 1) < HEAD_DIM
        ck_dup = _dup_halves(ck_ref[0], lo_c)
        cv_dup = _dup_halves(cv_ref[0], lo_c)
        for g in range(KV_HEADS):
            for b in range(n_blk):
                kd_scr[g, b] = k_dup[g][b * qb:(b + 1) * qb].astype(BF16)
                vt_scr[g, b] = v_dup[g][b * qb:(b + 1) * qb].T.astype(BF16)
            ckd_scr[g] = ck_dup[g].astype(BF16)
            cvt_scr[g] = cv_dup[g].T.astype(BF16)

    lo = lax.broadcasted_iota(jnp.int32, (qb, PAIR), 1) < HEAD_DIM
    f16 = (lax.broadcasted_iota(jnp.int32, (qb, PAIR), 1) % 32) < 16
    q0 = pl.multiple_of(i * qb, qb)
    cos = cos_ref[pl.ds(q0, qb), :]
    sin = sin_ref[pl.ds(q0, qb), :]
    qpos = i * qb + lax.broadcasted_iota(jnp.int32, (qb, 2 * qb), 1) % qb
    krow = lax.broadcasted_iota(jnp.int32, (qb, 2 * qb), 0)
    scale = HEAD_DIM ** -0.5

    pairs = range(N_HEADS // 2)
    kv_head = [(2 * p) // (N_HEADS // KV_HEADS) for p in pairs]
    win = []
    for jj in range(3):
        j = i - 1 + jj
        jc = jnp.clip(j, 0, n_blk - 1)
        kpos = jnp.where(j == jc, jc * qb, -4 * seq_len) + krow
        win.append((jc, jnp.abs(qpos - kpos) <= WINDOW))

    qs = [_stack_heads(_rope(q_ref[:, PAIR * p:PAIR * (p + 1)], cos, sin, f16) * scale,
                       lo).astype(BF16) for p in pairs]
    s_t = []
    for p in pairs:
        g = kv_head[p]
        nt = functools.partial(lax.dot_general, dimension_numbers=_NT, preferred_element_type=F32)
        parts = [nt(ckd_scr[g], qs[p])]
        for jc, ok in win:
            parts.append(jnp.where(ok, nt(kd_scr[g, jc], qs[p]), NEG_INF))
        s_t.append(jnp.concatenate(parts, 0))
    o_t = []
    for p in pairs:
        g = kv_head[p]
        v_t = jnp.concatenate([cvt_scr[g]] + [vt_scr[g, jc] for jc, _ in win], 1)
        o_t.append(_softmax_pv(s_t[p], _sink_row(sink_ref, p, qb), v_t))
    for p in pairs:
        o_ref[:, PAIR * p:PAIR * (p + 1)] = _unstack_heads_t(o_t[p], qb)


def _attn_lat(q, kv, ck, cv, layer, cos_t, sin_t, sink_rows, row0, n_seq, seq_len):
    qb = WINDOW
    n_blk = seq_len // qb
    blk0 = row0 // qb
    seq0 = row0 // seq_len
    n_past = ck.shape[2]
    kern = functools.partial(_attn_lat_kernel, seq_len=seq_len)
    return pl.pallas_call(
        kern,
        grid=(n_seq, n_blk),
        in_specs=[pl.BlockSpec((qb, ATT_Q), lambda b, i: (blk0 + b * n_blk + i, 0)),
                  pl.BlockSpec((seq_len, 2 * ATT_KV), lambda b, i: (seq0 + b, 0)),
                  pl.BlockSpec((1, None, n_past, ATT_KV), lambda b, i: (b, layer, 0, 0)),
                  pl.BlockSpec((1, None, n_past, ATT_KV), lambda b, i: (b, layer, 0, 0)),
                  pl.BlockSpec((seq_len, PAIR), lambda b, i: (0, 0)),
                  pl.BlockSpec((seq_len, PAIR), lambda b, i: (0, 0)),
                  _at_layer((N_HEADS, 128), layer)],
        out_specs=pl.BlockSpec((qb, ATT_Q), lambda b, i: (b * n_blk + i, 0)),
        out_shape=jax.ShapeDtypeStruct((n_seq * seq_len, ATT_Q), F32),
        scratch_shapes=[pltpu.VMEM((KV_HEADS, n_blk, qb, PAIR), BF16),
                        pltpu.VMEM((KV_HEADS, n_blk, PAIR, qb), BF16),
                        pltpu.VMEM((KV_HEADS, n_past, PAIR), BF16),
                        pltpu.VMEM((KV_HEADS, PAIR, n_past), BF16)],
        compiler_params=_params(1, 1),
        name="attn_lat",
    )(q, kv, ck, cv, cos_t, sin_t, sink_rows)


def _postmix_kernel(x_ref, octx_ref, olat_ref, *rest, with_router, n_ctx_tiles, per_slot):
    y_fwd = rest[:SCAN_SLOTS]
    y_bwd = rest[SCAN_SLOTS:2 * SCAN_SLOTS]
    (bonus_ref, g_ref, mod_ref, gpost_ref, gpre_ref, lng_ref, lnb_ref, wo_ref, bd_ref, router_ref,
     x1_ref, h_ref, gates_ref) = rest[2 * SCAN_SLOTS:]
    m = mod_ref[0]
    bd = bd_ref[...]
    tile = lax.broadcasted_iota(jnp.int32, octx_ref.shape, 0) * 0 + pl.program_id(0)
    o_att = jnp.where(tile < n_ctx_tiles, octx_ref[...], olat_ref[...])
    slot = (lax.broadcasted_iota(jnp.int32, y_fwd[0].shape, 0) * 0 + pl.program_id(0)) // per_slot
    y = y_fwd[0][...] + y_bwd[0][...]
    for sl in range(1, SCAN_SLOTS):
        y = jnp.where(slot == sl, y_fwd[sl][...] + y_bwd[sl][...], y)
    inv_n = 1.0 / R_HEAD
    mu = _head_sum(y, bd) * inv_n
    yc = y - mu
    var = _head_sum(yc * yc, bd) * inv_n
    yn = yc * lax.rsqrt(var + GN_EPS)
    o_rw = (yn * lng_ref[...] + lnb_ref[...] + bonus_ref[...]) * g_ref[...]
    o = (jnp.dot(o_att.astype(BF16), wo_ref[:ATT_Q, :], preferred_element_type=F32)
         + jnp.dot(o_rw.astype(BF16), wo_ref[ATT_Q:, :], preferred_element_type=F32))
    x1 = x_ref[...] + m[2:3] * (_rms(o) * gpost_ref[...])
    x1_ref[...] = x1
    h = (_rms(x1) * gpre_ref[...]) * (1.0 + m[4:5]) + m[3:4]
    h_ref[...] = h.astype(BF16)
    if with_router:
        h_hi, h_lo = _split2(h)
        r_hi, r_lo = _split2(router_ref[...])
        d = functools.partial(jnp.dot, preferred_element_type=F32)
        logits = d(h_hi, r_hi) + (d(h_hi, r_lo) + d(h_lo, r_hi))
        lane = lax.broadcasted_iota(jnp.int32, logits.shape, 1)
        valid = lane < N_EXPERTS
        logits = jnp.where(valid, logits, NEG_INF)
        e = jnp.exp(logits - jnp.max(logits, -1, keepdims=True))
        probs = e / jnp.sum(e, -1, keepdims=True)
        lane_f = lane.astype(F32)
        p1 = jnp.max(probs, -1, keepdims=True)
        i1 = jnp.min(jnp.where(probs == p1, lane_f, 1e9), -1, keepdims=True)
        rest = jnp.where(lane_f == i1, -1.0, probs)
        p2 = jnp.max(rest, -1, keepdims=True)
        i2 = jnp.min(jnp.where(rest == p2, lane_f, 1e9), -1, keepdims=True)
        sel = (lane_f == i1) | (lane_f == i2)
        gates_ref[...] = jnp.where(sel, probs / (p1 + p2), 0.0)
    else:
        gates_ref[...] = jnp.ones(gates_ref.shape, F32)


def _postmix(x, o_ctx, o_lat, y0, y1, bonus, g, mods, p, w_o, layer, consts, mod_map,
             router_pad, router_layer, n_ctx_tiles):
    nt = x.shape[0]
    row = lambda i: (i, 0)
    c2 = lambda i: (0, 0)
    with_router = router_layer is not None
    per_slot = nt // TILE // SCAN_SLOTS
    kern = functools.partial(_postmix_kernel, with_router=with_router, n_ctx_tiles=n_ctx_tiles,
                             per_slot=per_slot)
    n_lat_tiles = nt // TILE - n_ctx_tiles
    y_specs = [pl.BlockSpec((TILE, RW),
                            lambda i, sl=sl: (jnp.clip(i - sl * per_slot, 0, per_slot - 1), 0))
               for sl in range(SCAN_SLOTS)]
    return pl.pallas_call(
        kern,
        grid=(nt // TILE,),
        in_specs=[pl.BlockSpec((TILE, D_MODEL), row),
                  pl.BlockSpec((TILE, ATT_Q), lambda i: (jnp.minimum(i, n_ctx_tiles - 1), 0)),
                  pl.BlockSpec((TILE, ATT_Q),
                               lambda i: (jnp.clip(i - n_ctx_tiles, 0, n_lat_tiles - 1), 0)),
                  *y_specs, *y_specs,
                  pl.BlockSpec((TILE, RW), row),
                  pl.BlockSpec((TILE, RW), row),
                  pl.BlockSpec((None, 1, 6, D_MODEL), lambda i: (layer, mod_map(i), 0, 0)),
                  _at_layer((1, D_MODEL), layer),
                  _at_layer((1, D_MODEL), layer),
                  _at_layer((1, RW), layer),
                  _at_layer((1, RW), layer),
                  _at_layer((ATT_Q + RW, D_MODEL), layer),
                  pl.BlockSpec((RW, RW), c2),
                  _at_layer((D_MODEL, 128), router_layer or 0)],
        out_specs=[pl.BlockSpec((TILE, D_MODEL), row),
                   pl.BlockSpec((TILE, D_MODEL), row),
                   pl.BlockSpec((TILE, 128), row)],
        out_shape=[jax.ShapeDtypeStruct((nt, D_MODEL), F32),
                   jax.ShapeDtypeStruct((nt, D_MODEL), BF16),
                   jax.ShapeDtypeStruct((nt, 128), F32)],
        compiler_params=_params(1),
        name="postmix",
    )(x, o_ctx, o_lat, *y0, *y1, bonus, g, mods, p["g_post_mix"], p["g_pre_ffn"],
      p["ln_g"], p["ln_b"], w_o, consts["bd"], router_pad)


def _ffn_kernel(h_ref, x1_ref, gates_ref, mod_ref, gpost_ref, w1_ref, w3_ref, w2_ref, o_ref,
                *, n_blocks, experts_per_block):
    blk = pl.program_id(1)
    per_expert = w1_ref.shape[-1]
    width = experts_per_block * per_expert

    def cols(w_ref, lo, hi):
        pieces = []
        for j in range(experts_per_block):
            a, b = max(lo, j * per_expert), min(hi, (j + 1) * per_expert)
            if a < b:
                pieces.append(w_ref[j, :, a - j * per_expert:b - j * per_expert])
        return pieces[0] if len(pieces) == 1 else jnp.concatenate(pieces, 1)

    @pl.when(blk == 0)
    def _():
        o_ref[...] = jnp.zeros_like(o_ref)

    h = h_ref[...]
    lane = lax.broadcasted_iota(jnp.int32, gates_ref.shape, 1)
    gate = [jnp.sum(jnp.where(lane == blk * experts_per_block + j, gates_ref[...], 0.0), -1,
                    keepdims=True) for j in range(experts_per_block)]
    out = None
    for lo in range(0, width, FF_SUB):
        hi = min(lo + FF_SUB, width)
        a = jnp.dot(h, cols(w1_ref, lo, hi), preferred_element_type=F32)
        b = jnp.dot(h, cols(w3_ref, lo, hi), preferred_element_type=F32)
        col = lo + lax.broadcasted_iota(jnp.int32, (1, hi - lo), 1)
        g_col = gate[-1]
        for j in range(experts_per_block - 2, -1, -1):
            g_col = jnp.where(col < (j + 1) * per_expert, gate[j], g_col)
        t = ((a * _sigmoid(a)) * b) * g_col
        part = jnp.dot(t.astype(BF16), w2_ref[0, lo:hi, :], preferred_element_type=F32)
        out = part if out is None else out + part
    o_ref[...] += out

    @pl.when(blk == n_blocks - 1)
    def _():
        m = mod_ref[0]
        o_ref[...] = x1_ref[...] + m[5:6] * (_rms(o_ref[...]) * gpost_ref[...])


def _ffn(h, x1, gates, mods, g_post, w1, w3, w2, layer, w_layer, experts_per_block, mod_map_ffn):
    nt = h.shape[0]
    _, n_experts, _, per_expert = w1.shape
    n_blocks = n_experts // experts_per_block
    width = experts_per_block * per_expert
    up_block = (None, experts_per_block, D_MODEL, per_expert)
    row = lambda i, b: (i, 0)
    kern = functools.partial(_ffn_kernel, n_blocks=n_blocks, experts_per_block=experts_per_block)
    return pl.pallas_call(
        kern,
        grid=(nt // FFN_TILE, n_blocks),
        in_specs=[pl.BlockSpec((FFN_TILE, D_MODEL), row),
                  pl.BlockSpec((FFN_TILE, D_MODEL), row),
                  pl.BlockSpec((FFN_TILE, 128), row),
                  pl.BlockSpec((None, 1, 6, D_MODEL), lambda i, b: (layer, mod_map_ffn(i), 0, 0)),
                  _at_layer((1, D_MODEL), layer),
                  pl.BlockSpec(up_block, lambda i, b: (w_layer, b, 0, 0)),
                  pl.BlockSpec(up_block, lambda i, b: (w_layer, b, 0, 0)),
                  pl.BlockSpec((None, 1, width, D_MODEL), lambda i, b: (w_layer, b, 0, 0))],
        out_specs=pl.BlockSpec((FFN_TILE, D_MODEL), row),
        out_shape=jax.ShapeDtypeStruct((nt, D_MODEL), F32),
        compiler_params=_params(1, 1, FFN_VMEM_LIMIT),
        name="ffn",
    )(h, x1, gates, mods, g_post, w1, w3, w2)


def _rope_tables(seq_len):
    half = HEAD_DIM // 2
    nf = half // 2
    inv = ROPE_THETA ** (-jnp.arange(nf, dtype=F32) / nf)
    t = jnp.arange(seq_len)
    row = (t // GRID_W).astype(F32)
    col = (t % GRID_W).astype(F32)
    lane = jnp.arange(PAIR)
    pos = jnp.where(((lane % HEAD_DIM) // half == 0)[None, :], row[:, None], col[:, None])
    ang = pos * inv[lane % nf][None, :]
    first = ((lane % half) < nf)[None, :]
    return jnp.cos(ang), jnp.where(first, -jnp.sin(ang), jnp.sin(ang))


def _constants():
    r = jnp.arange(RW)
    bd = (r[:, None] // R_HEAD == r[None, :] // R_HEAD).astype(BF16)
    t = jnp.arange(TILE)
    same = t[:, None] // CHUNK == t[None, :] // CHUNK
    tri = jnp.stack([same & (t[None, :] <= t[:, None]), same & (t[None, :] >= t[:, None])])
    return {"bd": bd, "tri": tri.astype(BF16)}


def _pad_rows(w2):
    z = jnp.zeros_like(w2[:, 0])
    return jnp.stack([jnp.concatenate([w2[:, 0], z], 1), jnp.concatenate([z, w2[:, 1]], 1)],
                     1).astype(BF16)


def kernel(x_prompt, x_sample, cache_k, cache_v, state_rwkv, c, c_ctx, w_mod, b_mod, g_pre_mix, g_post_mix, g_pre_ffn, g_post_ffn, w_in, mu_shift, w_o, attn_sink, rw_w0, rw_w1, rw_w2, rw_a0, rw_a1, rw_a2, rw_g1, rw_g2, rw_k_k, rw_k_a, rw_r_k, rw_ln_g, rw_ln_b, ffn_w1, ffn_w3, ffn_w2, moe_router, moe_w1, moe_w3, moe_w2):
    n_ctx, s_ctx, _ = x_prompt.shape
    n_lat, s_lat, _ = x_sample.shape
    n_past = cache_k.shape[2]
    assert s_ctx == TILE and s_lat % TILE == 0 and (n_ctx * s_ctx) % s_lat == 0
    nt_ctx = n_ctx * s_ctx
    n_ctx_tiles = nt_ctx // TILE
    tiles_per_lat = s_lat // TILE
    mod_map = functools.partial(_mod_row, n_ctx_tiles=n_ctx_tiles, tiles_per_lat=tiles_per_lat)
    ffn_per = FFN_TILE // TILE
    mod_map_ffn = lambda i: mod_map(i * ffn_per)

    x = jnp.concatenate([x_prompt.reshape(nt_ctx, D_MODEL), x_sample.reshape(n_lat * s_lat, D_MODEL)], 0)
    n_cond = 8
    cond = jnp.concatenate([c_ctx[None, :], c, jnp.zeros((n_cond - 1 - n_lat, D_MODEL), F32)], 0)
    mods = _modulation(cond, w_mod, b_mod).reshape(DEPTH, n_cond, 6, D_MODEL)

    consts = _constants()
    cos_t, sin_t = _rope_tables(s_lat)
    w_cat = jnp.concatenate(
        [w_in, rw_w1[:, 0], rw_w1[:, 1], rw_a1[:, 0], rw_a1[:, 1], rw_g1], 2).astype(BF16)
    w_o_b = w_o.astype(BF16)
    dense_w = [w.astype(BF16)[:, None] for w in (ffn_w1, ffn_w3, ffn_w2)]
    moe_w = [moe_w1.astype(BF16), moe_w3.astype(BF16),
             moe_w2.astype(BF16).reshape(moe_w2.shape[0], N_EXPERTS // 2, 2 * D_FF_EXPERT, D_MODEL)]
    router_pad = jnp.pad(moe_router, ((0, 0), (0, 0), (0, 128 - N_EXPERTS)))
    sink_rows = jnp.broadcast_to(attn_sink[:, :, None], (DEPTH, N_HEADS, 128))
    ck = cache_k.reshape(n_lat, DEPTH, n_past, ATT_KV)
    cv = cache_v.reshape(n_lat, DEPTH, n_past, ATT_KV)

    vecs = lambda a: a.reshape(DEPTH, 1, -1)
    p = {"mu": vecs(mu_shift), "w0": rw_w0, "a0": rw_a0, "w2": _pad_rows(rw_w2),
         "a2": _pad_rows(rw_a2), "g2": rw_g2.astype(BF16), "k_k": vecs(rw_k_k),
         "k_a": vecs(rw_k_a), "r_k": vecs(rw_r_k), "ln_g": vecs(rw_ln_g), "ln_b": vecs(rw_ln_b),
         "g_post_mix": vecs(g_post_mix), "g_pre_ffn": vecs(g_pre_ffn)}
    g_pre_mix_v = vecs(g_pre_mix)
    g_post_ffn_v = vecs(g_post_ffn)

    new_k, new_v, new_s = [], [], []
    for l in range(DEPTH):
        q, kv, v_b, al, be, ka, rh, wl, bonus, g = _inprep(
            x, mods, g_pre_mix_v, w_cat, p, l, consts, mod_map, n_ctx_tiles, tiles_per_lat)
        y0, y1, s_ctx_new = _scan(al, be, ka, rh, v_b, wl, state_rwkv[:, l],
                                  n_ctx_tiles, tiles_per_lat)
        o_ctx = _attn_ctx(q, kv, sink_rows, l, n_ctx, s_ctx)
        o_lat = _attn_lat(q, kv, ck, cv, l, cos_t, sin_t, sink_rows, nt_ctx, n_lat, s_lat)

        i = l // 2
        moe = l % 2 == 1
        x1, h, gates = _postmix(x, o_ctx, o_lat, y0, y1, bonus, g, mods, p, w_o_b, l, consts,
                                mod_map, router_pad, i if moe else None, n_ctx_tiles)
        w1, w3, w2 = moe_w if moe else dense_w
        x = _ffn(h, x1, gates, mods, g_post_ffn_v, w1, w3, w2, l, i, 2 if moe else 1,
                 mod_map_ffn)

        new_k.append(kv[:nt_ctx, :ATT_KV].reshape(n_ctx, s_ctx, KV_HEADS, HEAD_DIM))
        new_v.append(kv[:nt_ctx, ATT_KV:].reshape(n_ctx, s_ctx, KV_HEADS, HEAD_DIM))
        new_s.append(s_ctx_new)

    y_prompt = x[:nt_ctx].reshape(n_ctx, s_ctx, D_MODEL)
    y_sample = x[nt_ctx:].reshape(n_lat, s_lat, D_MODEL)
    return (y_prompt, y_sample, jnp.stack(new_k, 1), jnp.stack(new_v, 1), jnp.stack(new_s, 1))
```

```python
import functools

import jax
import jax.numpy as jnp
from jax import lax
from jax.experimental import pallas as pl
from jax.experimental.pallas import tpu as pltpu

F32 = jnp.float32
BF16 = jnp.bfloat16

D_MODEL = 1024
DEPTH = 4
GRID_W = 64
N_HEADS = 8
KV_HEADS = 2
HEAD_DIM = 64
WINDOW = 128
ROPE_THETA = 10000.0
R_HEADS = 8
R_HEAD = 64
RW = R_HEADS * R_HEAD
LORA_W = 64
LORA_A = 64
LORA_G = 128
LORA_ALL = 2 * LORA_W + 2 * LORA_A + LORA_G
ATT_Q = N_HEADS * HEAD_DIM
ATT_KV = KV_HEADS * HEAD_DIM
D_IN = ATT_Q + 2 * ATT_KV + 3 * RW
D_FF = 2816
N_EXPERTS = 8
D_FF_EXPERT = 1408
NORM_EPS = 1e-6
GN_EPS = 64e-5
NEG_INF = -1e30

TILE = 256
CHUNK = 64
PAIR = 2 * R_HEAD
N_PAIRS = R_HEADS // 2
SCAN_SLOTS = 3
FFN_TILE = 512
FF_SUB = 1024
MOD_COLS = 1536
VMEM_LIMIT = 48 * 1024 * 1024
FFN_VMEM_LIMIT = 58 * 1024 * 1024

_NT = (((1,), (1,)), ((), ()))


def _params(n_parallel, n_arbitrary=0, vmem_limit=VMEM_LIMIT):
    sem = ("parallel",) * n_parallel + ("arbitrary",) * n_arbitrary
    return pltpu.CompilerParams(dimension_semantics=sem, vmem_limit_bytes=vmem_limit)


def _sigmoid(x):
    return 0.5 * jnp.tanh(0.5 * x) + 0.5


def _mm(a, b):
    return jnp.dot(a.astype(BF16), b.astype(BF16), preferred_element_type=F32)


def _mm_nt(a, b):
    return lax.dot_general(a.astype(BF16), b.astype(BF16), _NT, preferred_element_type=F32)


def _split2(x):
    hi = x.astype(BF16)
    return hi, (x - hi.astype(F32)).astype(BF16)


def _head_sum(x, ones_bf16):
    return jnp.dot(x.astype(BF16), ones_bf16, preferred_element_type=F32)


def _mm_ones_left(ones_bf16, x):
    hi, lo = _split2(x)
    d = functools.partial(jnp.dot, preferred_element_type=F32)
    return d(ones_bf16, hi) + d(ones_bf16, lo)


def _rms(x):
    return x * lax.rsqrt(jnp.mean(x * x, -1, keepdims=True) + NORM_EPS)


def _at_layer(shape, layer):
    zeros = (0,) * len(shape)
    return pl.BlockSpec((None,) + tuple(shape), lambda *_: (layer,) + zeros)


def _mod_row(i, n_ctx_tiles, tiles_per_lat):
    return jnp.where(i < n_ctx_tiles, 0, 1 + (i - n_ctx_tiles) // tiles_per_lat)


def _mod_kernel(cond_ref, w_ref, b_ref, o_ref):
    c = cond_ref[...]
    s = c * _sigmoid(c)
    o_ref[0] = _mm(s, w_ref[0]) + b_ref[0]


def _modulation(cond, w_mod, b_mod):
    n_cond = cond.shape[0]
    n_col = 6 * D_MODEL // MOD_COLS
    return pl.pallas_call(
        _mod_kernel,
        grid=(DEPTH, n_col),
        in_specs=[pl.BlockSpec((n_cond, D_MODEL), lambda l, j: (0, 0)),
                  pl.BlockSpec((1, D_MODEL, MOD_COLS), lambda l, j: (l, 0, j)),
                  pl.BlockSpec((1, 1, MOD_COLS), lambda l, j: (l, 0, j))],
        out_specs=pl.BlockSpec((1, n_cond, MOD_COLS), lambda l, j: (l, 0, j)),
        out_shape=jax.ShapeDtypeStruct((DEPTH, n_cond, 6 * D_MODEL), F32),
        compiler_params=_params(2),
        name="modulation",
    )(cond, w_mod, b_mod.reshape(DEPTH, 1, 6 * D_MODEL))


HALO = 8


def _inprep_kernel(x_ref, xp_ref, xn_ref, mod_ref, g_ref, w_ref, mu_ref, w0_ref, w2_ref, a0_ref,
                   a2_ref, g2_ref, kk_ref, ka_ref, rk_ref, bd_ref, tri_ref,
                   q_out, kv_out, v_out, pk_out, wl_out, bonus_out, g_out,
                   *, n_ctx_tiles, tiles_per_lat):
    i = pl.program_id(0)
    j = (i - n_ctx_tiles) % tiles_per_lat
    is_ctx = i < n_ctx_tiles
    first = is_ctx | (j == 0)
    last = is_ctx | (j == tiles_per_lat - 1)

    m = mod_ref[0]
    x_all = jnp.concatenate([xp_ref[...], x_ref[...], xn_ref[...]], 0)
    h = (_rms(x_all) * g_ref[...]) * (1.0 + m[1:2]) + m[0:1]
    main = slice(HALO, HALO + TILE)
    h_main = h[main].astype(BF16)
    n_att = ATT_Q + 2 * ATT_KV
    u_all = jnp.dot(h.astype(BF16), w_ref[:, n_att:], preferred_element_type=F32)
    rkv = slice(0, D_IN - n_att)
    u = u_all[main, rkv]
    lora = u_all[main, D_IN - n_att:]

    row = lax.broadcasted_iota(jnp.int32, (TILE, 1), 0)
    p_row = u_all[HALO - 1:HALO, rkv] * (1.0 - first.astype(F32))
    n_row = u_all[HALO + TILE:HALO + TILE + 1, rkv] * (1.0 - last.astype(F32))
    prev = jnp.where(row == 0, p_row, pltpu.roll(u, 1, 0))
    nxt = jnp.where(row == TILE - 1, n_row, pltpu.roll(u, TILE - 1, 0))
    us = u + mu_ref[...] * (0.5 * (prev + nxt) - u)
    r = us[:, :RW]
    k = us[:, RW:2 * RW]
    v = us[:, 2 * RW:]

    bd = bd_ref[...]
    kk = k * kk_ref[...]
    kk = kk * lax.rsqrt(_head_sum(kk * kk, bd) + 1e-12)

    lw_in = jnp.tanh(lora[:, :2 * LORA_W])
    la_in = lora[:, 2 * LORA_W:2 * LORA_W + 2 * LORA_A]
    kd_sum = jnp.zeros_like(k)
    for d in range(2):
        w_pre = w0_ref[d:d + 1, :] + _mm(lw_in, w2_ref[d])
        lw = -jnp.exp(-0.5) * _sigmoid(w_pre)
        a = _sigmoid(a0_ref[d:d + 1, :] + _mm(la_in, a2_ref[d]))
        kd = k * (1.0 + (a - 1.0) * ka_ref[...])
        kd_sum = kd_sum + kd
        cw = _mm_ones_left(tri_ref[d], lw)
        e_neg = jnp.exp(-cw)
        pk_out[d, :, 0 * RW:1 * RW] = (kk * jnp.exp(cw - lw)).astype(BF16)
        pk_out[d, :, 1 * RW:2 * RW] = (kk * a * e_neg).astype(BF16)
        pk_out[d, :, 2 * RW:3 * RW] = (kd * e_neg).astype(BF16)
        pk_out[d, :, 3 * RW:4 * RW] = (r * jnp.exp(cw)).astype(BF16)
        for c in range(TILE // CHUNK):
            end = c * CHUNK + (CHUNK - 1 if d == 0 else 0)
            wl_out[0, 4 * d + c:4 * d + c + 1, :] = jnp.exp(cw[end:end + 1, :])
    v_out[...] = v.astype(BF16)
    bonus_out[...] = _head_sum(r * kd_sum * rk_ref[...], bd) * v
    g_out[...] = _mm(_sigmoid(lora[:, 2 * LORA_W + 2 * LORA_A:]), g2_ref[...])
    u_att = jnp.dot(h_main, w_ref[:, :n_att], preferred_element_type=F32)
    q_out[...] = u_att[:, :ATT_Q]
    kv_out[...] = u_att[:, ATT_Q:]


def _inprep(x, mods, g, w_cat, p, layer, consts, mod_map, n_ctx_tiles, tiles_per_lat):
    nt = x.shape[0]
    n_tiles = nt // TILE
    row = lambda i: (i, 0)
    drow = lambda i: (0, i, 0)
    hb = TILE // HALO
    kern = functools.partial(_inprep_kernel, n_ctx_tiles=n_ctx_tiles, tiles_per_lat=tiles_per_lat)
    dshape = jax.ShapeDtypeStruct((2, nt, 4 * RW), BF16)
    return pl.pallas_call(
        kern,
        grid=(n_tiles,),
        in_specs=[pl.BlockSpec((TILE, D_MODEL), row),
                  pl.BlockSpec((HALO, D_MODEL), lambda i: (jnp.maximum(i * hb - 1, 0), 0)),
                  pl.BlockSpec((HALO, D_MODEL),
                               lambda i: (jnp.minimum((i + 1) * hb, nt // HALO - 1), 0)),
                  pl.BlockSpec((None, 1, 6, D_MODEL), lambda i: (layer, mod_map(i), 0, 0)),
                  _at_layer((1, D_MODEL), layer),
                  _at_layer((D_MODEL, D_IN + LORA_ALL), layer),
                  _at_layer((1, 3 * RW), layer),
                  _at_layer((2, RW), layer),
                  _at_layer((2, 2 * LORA_W, RW), layer),
                  _at_layer((2, RW), layer),
                  _at_layer((2, 2 * LORA_A, RW), layer),
                  _at_layer((LORA_G, RW), layer),
                  _at_layer((1, RW), layer),
                  _at_layer((1, RW), layer),
                  _at_layer((1, RW), layer),
                  pl.BlockSpec((RW, RW), lambda i: (0, 0)),
                  pl.BlockSpec((2, TILE, TILE), lambda i: (0, 0, 0))],
        out_specs=[pl.BlockSpec((TILE, ATT_Q), row),
                   pl.BlockSpec((TILE, 2 * ATT_KV), row),
                   pl.BlockSpec((TILE, RW), row),
                   pl.BlockSpec((2, TILE, 4 * RW), drow),
                   pl.BlockSpec((1, 8, RW), lambda i: (i, 0, 0)),
                   pl.BlockSpec((TILE, RW), row),
                   pl.BlockSpec((TILE, RW), row)],
        out_shape=[jax.ShapeDtypeStruct((nt, ATT_Q), F32),
                   jax.ShapeDtypeStruct((nt, 2 * ATT_KV), F32),
                   jax.ShapeDtypeStruct((nt, RW), BF16), dshape,
                   jax.ShapeDtypeStruct((n_tiles, 8, RW), F32),
                   jax.ShapeDtypeStruct((nt, RW), F32),
                   jax.ShapeDtypeStruct((nt, RW), F32)],
        compiler_params=_params(1),
        name="inproj_prep",
    )(x, x, x, mods, g, w_cat, p["mu"], p["w0"], p["w2"], p["a0"], p["a2"], p["g2"],
      p["k_k"], p["k_a"], p["r_k"], consts["bd"], consts["tri"])


def _scan_kernel(*refs, n_ctx_tiles, tiles_per_lat, per_slot):
    n_in = 7
    slots = range(SCAN_SLOTS)
    ins = [refs[n_in * sl:n_in * (sl + 1)] for sl in slots]
    outs = [refs[SCAN_SLOTS * n_in + 3 * sl:SCAN_SLOTS * n_in + 3 * (sl + 1)] for sl in slots]
    s_scr = refs[-1]
    for sl in slots:
        tile = sl * per_slot + pl.program_id(0)
        is_ctx = tile < n_ctx_tiles
        seq_start = (tile - n_ctx_tiles) % tiles_per_lat == 0

        @pl.when(is_ctx)
        def _():
            s_scr[sl] = jnp.zeros(s_scr.shape[1:], F32)

        @pl.when(jnp.logical_not(is_ctx) & seq_start)
        def _():
            s_scr[sl] = ins[sl][6][...]

    lo = lax.broadcasted_iota(jnp.int32, (CHUNK, PAIR), 1) < R_HEAD
    rt = lax.broadcasted_iota(jnp.int32, (CHUNK, PAIR), 0)
    ct = lax.broadcasted_iota(jnp.int32, (CHUNK, PAIR), 1) % CHUNK
    before = [ct < rt, ct > rt]
    upto = [ct <= rt, ct >= rt]
    same_head = (lax.broadcasted_iota(jnp.int32, (PAIR, PAIR), 0) // R_HEAD
                 == lax.broadcasted_iota(jnp.int32, (PAIR, PAIR), 1) // R_HEAD)
    n_sq = CHUNK.bit_length() - 2
    n_chunks = TILE // CHUNK

    def diag(x):
        x = x.astype(BF16)
        zero = jnp.zeros_like(x)
        return jnp.concatenate([jnp.where(lo, x, zero), jnp.where(lo, zero, x)], 0)

    def chunk(c, carry):
        chains = []
        for sl in slots:
            for d in range(2):
                pk, v, wl = ins[sl][3 * d:3 * (d + 1)]
                y_ref = outs[sl][d]
                ce = c if d == 0 else n_chunks - 1 - c
                rows = pl.ds(pl.multiple_of(ce * CHUNK, CHUNK), CHUNK)
                wl_all = wl[pl.ds(n_chunks * d + ce, 1), :]
                for p in range(N_PAIRS):
                    lanes = slice(PAIR * p, PAIR * (p + 1))
                    part = lambda q, p=p: slice(q * RW + PAIR * p, q * RW + PAIR * (p + 1))
                    chains.append(dict(
                        sl=sl, d=d, p=p, rows=rows, lanes=lanes, y_ref=y_ref,
                        wl=wl_all[:, lanes], a=pk[rows, part(0)], b=pk[rows, part(1)],
                        k=pk[rows, part(2)], r=pk[rows, part(3)], v=v[rows, lanes],
                        s=s_scr[sl, d, p]))

        zero = jnp.zeros((CHUNK, PAIR), F32)
        for ch in chains:
            ar = jnp.concatenate([ch["a"], ch["r"]], 0)
            ch["v_d"] = diag(ch["v"])
            bks = jnp.concatenate([diag(ch["b"]), diag(ch["k"]), ch["s"].astype(BF16)], 0)
            ch["gram"] = lax.dot_general(ar, bks, _NT, preferred_element_type=F32)
        for ch in chains:
            g, d = ch["gram"], ch["d"]
            ch["m_ab"] = jnp.where(before[d], g[:CHUNK, :PAIR], zero)
            ch["m_rb"] = jnp.where(upto[d], g[CHUNK:, :PAIR], zero)
            m_ak = jnp.where(before[d], g[:CHUNK, PAIR:2 * PAIR], zero)
            m_rk = jnp.where(upto[d], g[CHUNK:, PAIR:2 * PAIR], zero)
            ch["mv"] = _mm(jnp.concatenate([m_ak, m_rk], 0), ch["v_d"])

        for ch in chains:
            ch["x"] = ch["gram"][:CHUNK, 2 * PAIR:] + ch["mv"][:CHUNK]
            ch["pw"] = ch["m_ab"]
        for lvl in range(n_sq + 1):
            for ch in chains:
                if lvl < n_sq:
                    px = _mm(ch["pw"], jnp.concatenate([diag(ch["pw"]), diag(ch["x"])], 1))
                    ch["pw"] = px[:, :PAIR]
                    step = px[:, PAIR:]
                else:
                    step = _mm(ch["pw"], diag(ch["x"]))
                ch["x"] = ch["x"] - step if lvl == 0 else ch["x"] + step

        for ch in chains:
            z = -ch["x"]
            ch["y"] = ch["gram"][CHUNK:, 2 * PAIR:] + ch["mv"][CHUNK:] + _mm(ch["m_rb"], diag(z))
            ch["zv_t"] = jnp.concatenate([z, ch["v"].astype(F32)], 0).T
        for ch in chains:
            bk = jnp.concatenate([ch["b"], ch["k"]], 0)
            upd = jnp.where(same_head, _mm(ch["zv_t"], bk), 0.0)
            ch["s_new"] = (ch["s"] + upd) * ch["wl"]
        for ch in chains:
            ch["y_ref"][ch["rows"], ch["lanes"]] = ch["y"]
            s_scr[ch["sl"], ch["d"], ch["p"]] = ch["s_new"]
        return carry

    lax.fori_loop(0, n_chunks, chunk, 0)
    for sl in slots:
        st_ref = outs[sl][2]
        for d in range(2):
            for p in range(N_PAIRS):
                s = s_scr[sl, d, p]
                st_ref[d, 2 * p] = s[:R_HEAD, :R_HEAD]
                st_ref[d, 2 * p + 1] = s[R_HEAD:, R_HEAD:]


def _expand_state(s0):
    n = s0.shape[0]
    s = s0.reshape(n, 2, N_PAIRS, 2, R_HEAD, R_HEAD)
    z = jnp.zeros_like(s[:, :, :, 0])
    top = jnp.concatenate([s[:, :, :, 0], z], -1)
    bot = jnp.concatenate([z, s[:, :, :, 1]], -1)
    return jnp.concatenate([top, bot], -2)


def _scan(pk, v, wl, s0_lat, n_ctx_tiles, tiles_per_lat):
    nt = v.shape[0]
    n_tiles = nt // TILE
    per_slot = n_tiles // SCAN_SLOTS
    n_lat = s0_lat.shape[0]
    assert n_tiles % SCAN_SLOTS == 0
    for t0 in range(per_slot, n_tiles, per_slot):
        assert t0 <= n_ctx_tiles or (t0 - n_ctx_tiles) % tiles_per_lat == 0

    def mirror(t):
        j = (t - n_ctx_tiles) % tiles_per_lat
        return jnp.where(t < n_ctx_tiles, t, t - j + (tiles_per_lat - 1 - j))

    lat_seq = lambda t: jnp.clip((t - n_ctx_tiles) // tiles_per_lat, 0, n_lat - 1)
    state_in = (None, 2, N_PAIRS, PAIR, PAIR)
    state_out = (None, 2, R_HEADS, R_HEAD, R_HEAD)
    in_specs, out_specs, out_shape, n_ctx_slot = [], [], [], []
    for sl in range(SCAN_SLOTS):
        t0 = sl * per_slot
        fwd3 = pl.BlockSpec((None, TILE, 4 * RW), lambda i, t0=t0: (0, t0 + i, 0))
        bwd3 = pl.BlockSpec((None, TILE, 4 * RW), lambda i, t0=t0: (1, mirror(t0 + i), 0))
        fwd2 = pl.BlockSpec((TILE, RW), lambda i, t0=t0: (t0 + i, 0))
        bwd2 = pl.BlockSpec((TILE, RW), lambda i, t0=t0: (mirror(t0 + i), 0))
        wl_f = pl.BlockSpec((None, 8, RW), lambda i, t0=t0: (t0 + i, 0, 0))
        wl_b = pl.BlockSpec((None, 8, RW), lambda i, t0=t0: (mirror(t0 + i), 0, 0))
        in_specs += [fwd3, fwd2, wl_f, bwd3, bwd2, wl_b,
                     pl.BlockSpec(state_in, lambda i, t0=t0: (lat_seq(t0 + i), 0, 0, 0, 0))]
        n_c = min(max(n_ctx_tiles - t0, 0), per_slot)
        n_park = 1 if n_c < per_slot else 0
        n_ctx_slot.append(n_c)
        out_specs += [pl.BlockSpec((TILE, RW), lambda i: (i, 0)),
                      pl.BlockSpec((TILE, RW), lambda i, t0=t0: (mirror(t0 + i) - t0, 0)),
                      pl.BlockSpec(state_out, lambda i, n_c=n_c: (jnp.minimum(i, n_c), 0, 0, 0, 0))]
        out_shape += [jax.ShapeDtypeStruct((per_slot * TILE, RW), F32),
                      jax.ShapeDtypeStruct((per_slot * TILE, RW), F32),
                      jax.ShapeDtypeStruct((n_c + n_park, 2, R_HEADS, R_HEAD, R_HEAD), F32)]
    kern = functools.partial(_scan_kernel, n_ctx_tiles=n_ctx_tiles, tiles_per_lat=tiles_per_lat,
                             per_slot=per_slot)
    s0 = _expand_state(s0_lat)
    args = (pk, v, wl, pk, v, wl, s0)
    outs = pl.pallas_call(
        kern,
        grid=(per_slot,),
        in_specs=in_specs,
        out_specs=out_specs,
        out_shape=out_shape,
        scratch_shapes=[pltpu.VMEM((SCAN_SLOTS, 2, N_PAIRS, PAIR, PAIR), F32)],
        compiler_params=_params(0, 1),
        name="wkv_scan",
    )(*(args * SCAN_SLOTS))
    st = jnp.concatenate([outs[3 * sl + 2][:n_c] for sl, n_c in enumerate(n_ctx_slot) if n_c], 0)
    return outs[0::3], outs[1::3], st


def _dup_halves(x, lo):
    xr = pltpu.roll(x, HEAD_DIM, 1)
    return jnp.where(lo, x, xr), jnp.where(lo, xr, x)


def _stack_heads(qp, lo):
    zero = jnp.zeros_like(qp)
    return jnp.concatenate([jnp.where(lo, qp, zero), jnp.where(lo, zero, qp)], 0)


def _sink_row(sink_ref, p, rows):
    c = lax.broadcasted_iota(jnp.int32, (1, 2 * rows), 1)
    return jnp.where(c < rows, sink_ref[2 * p:2 * p + 1, 0:1], sink_ref[2 * p + 1:2 * p + 2, 0:1])


def _softmax_pv(s_t, sk, v_t):
    m = jnp.maximum(jnp.max(s_t, 0, keepdims=True), sk)
    e = jnp.exp(s_t - m)
    den = jnp.sum(e, 0, keepdims=True) + jnp.exp(sk - m)
    return jnp.dot(v_t, e.astype(BF16), preferred_element_type=F32) / den


def _unstack_heads_t(o_t, rows):
    top = lax.broadcasted_iota(jnp.int32, (PAIR, rows), 0) < HEAD_DIM
    return jnp.where(top, o_t[:, :rows], o_t[:, rows:]).T


def _attn_ctx_kernel(q_ref, kv_ref, sink_ref, o_ref):
    rows = q_ref.shape[0]
    lo = lax.broadcasted_iota(jnp.int32, (rows, PAIR), 1) < HEAD_DIM
    k_dup = [k.astype(BF16) for k in _dup_halves(kv_ref[:, :ATT_KV], lo)]
    v_t = [v.T.astype(BF16) for v in _dup_halves(kv_ref[:, ATT_KV:], lo)]
    scale = HEAD_DIM ** -0.5
    pairs = range(N_HEADS // 2)
    kv_head = [(2 * p) // (N_HEADS // KV_HEADS) for p in pairs]
    qs = [_stack_heads(q_ref[:, PAIR * p:PAIR * (p + 1)] * scale, lo).astype(BF16) for p in pairs]
    s_t = [lax.dot_general(k_dup[kv_head[p]], qs[p], _NT, preferred_element_type=F32)
           for p in pairs]
    o_t = [_softmax_pv(s_t[p], _sink_row(sink_ref, p, rows), v_t[kv_head[p]]) for p in pairs]
    for p in pairs:
        o_ref[:, PAIR * p:PAIR * (p + 1)] = _unstack_heads_t(o_t[p], rows)


def _attn_ctx(q, kv, sink_rows, layer, n_seq, seq_len):
    return pl.pallas_call(
        _attn_ctx_kernel,
        grid=(n_seq,),
        in_specs=[pl.BlockSpec((seq_len, ATT_Q), lambda b: (b, 0)),
                  pl.BlockSpec((seq_len, 2 * ATT_KV), lambda b: (b, 0)),
                  _at_layer((N_HEADS, 128), layer)],
        out_specs=pl.BlockSpec((seq_len, ATT_Q), lambda b: (b, 0)),
        out_shape=jax.ShapeDtypeStruct((n_seq * seq_len, ATT_Q), F32),
        compiler_params=_params(1),
        name="attn_ctx",
    )(q, kv, sink_rows)


def _rope(x, cos, sgn_sin, first16):
    sw = jnp.where(first16, pltpu.roll(x, PAIR - 16, 1), pltpu.roll(x, 16, 1))
    return x * cos + sw * sgn_sin


def _attn_lat_kernel(q_ref, kv_ref, ck_ref, cv_ref, cos_ref, sin_ref, sink_ref, o_ref,
                     kd_scr, vt_scr, ckd_scr, cvt_scr, *, seq_len):
    i = pl.program_id(1)
    qb = q_ref.shape[0]
    n_blk = seq_len // qb

    @pl.when(i == 0)
    def _():
        lo_s = lax.broadcasted_iota(jnp.int32, (seq_len, PAIR), 1) < HEAD_DIM
        f16_s = (lax.broadcasted_iota(jnp.int32, (seq_len, PAIR), 1) % 32) < 16
        k_r = _rope(kv_ref[:, :ATT_KV], cos_ref[...], sin_ref[...], f16_s)
        k_dup = _dup_halves(k_r, lo_s)
        v_dup = _dup_halves(kv_ref[:, ATT_KV:], lo_s)
        lo_c = lax.broadcasted_iota(jnp.int32, ck_ref.shape[1:], 1) < HEAD_DIM
        ck_dup = _dup_halves(ck_ref[0], lo_c)
        cv_dup = _dup_halves(cv_ref[0], lo_c)
        for g in range(KV_HEADS):
            for b in range(n_blk):
                kd_scr[g, b] = k_dup[g][b * qb:(b + 1) * qb].astype(BF16)
                vt_scr[g, b] = v_dup[g][b * qb:(b + 1) * qb].T.astype(BF16)
            ckd_scr[g] = ck_dup[g].astype(BF16)
            cvt_scr[g] = cv_dup[g].T.astype(BF16)

    lo = lax.broadcasted_iota(jnp.int32, (qb, PAIR), 1) < HEAD_DIM
    f16 = (lax.broadcasted_iota(jnp.int32, (qb, PAIR), 1) % 32) < 16
    q0 = pl.multiple_of(i * qb, qb)
    cos = cos_ref[pl.ds(q0, qb), :]
    sin = sin_ref[pl.ds(q0, qb), :]
    qpos = i * qb + lax.broadcasted_iota(jnp.int32, (qb, 2 * qb), 1) % qb
    krow = lax.broadcasted_iota(jnp.int32, (qb, 2 * qb), 0)
    scale = HEAD_DIM ** -0.5

    pairs = range(N_HEADS // 2)
    kv_head = [(2 * p) // (N_HEADS // KV_HEADS) for p in pairs]
    win = []
    for jj in range(3):
        j = i - 1 + jj
        jc = jnp.clip(j, 0, n_blk - 1)
        kpos = jnp.where(j == jc, jc * qb, -4 * seq_len) + krow
        win.append((jc, jnp.abs(qpos - kpos) <= WINDOW))

    qs = [_stack_heads(_rope(q_ref[:, PAIR * p:PAIR * (p + 1)], cos, sin, f16) * scale,
                       lo).astype(BF16) for p in pairs]
    s_t = []
    for p in pairs:
        g = kv_head[p]
        nt = functools.partial(lax.dot_general, dimension_numbers=_NT, preferred_element_type=F32)
        parts = [nt(ckd_scr[g], qs[p])]
        for jc, ok in win:
            parts.append(jnp.where(ok, nt(kd_scr[g, jc], qs[p]), NEG_INF))
        s_t.append(jnp.concatenate(parts, 0))
    o_t = []
    for p in pairs:
        g = kv_head[p]
        v_t = jnp.concatenate([cvt_scr[g]] + [vt_scr[g, jc] for jc, _ in win], 1)
        o_t.append(_softmax_pv(s_t[p], _sink_row(sink_ref, p, qb), v_t))
    for p in pairs:
        o_ref[:, PAIR * p:PAIR * (p + 1)] = _unstack_heads_t(o_t[p], qb)


def _attn_lat(q, kv, ck, cv, layer, cos_t, sin_t, sink_rows, row0, n_seq, seq_len):
    qb = WINDOW
    n_blk = seq_len // qb
    blk0 = row0 // qb
    seq0 = row0 // seq_len
    n_past = ck.shape[2]
    kern = functools.partial(_attn_lat_kernel, seq_len=seq_len)
    return pl.pallas_call(
        kern,
        grid=(n_seq, n_blk),
        in_specs=[pl.BlockSpec((qb, ATT_Q), lambda b, i: (blk0 + b * n_blk + i, 0)),
                  pl.BlockSpec((seq_len, 2 * ATT_KV), lambda b, i: (seq0 + b, 0)),
                  pl.BlockSpec((1, None, n_past, ATT_KV), lambda b, i: (b, layer, 0, 0)),
                  pl.BlockSpec((1, None, n_past, ATT_KV), lambda b, i: (b, layer, 0, 0)),
                  pl.BlockSpec((seq_len, PAIR), lambda b, i: (0, 0)),
                  pl.BlockSpec((seq_len, PAIR), lambda b, i: (0, 0)),
                  _at_layer((N_HEADS, 128), layer)],
        out_specs=pl.BlockSpec((qb, ATT_Q), lambda b, i: (b * n_blk + i, 0)),
        out_shape=jax.ShapeDtypeStruct((n_seq * seq_len, ATT_Q), F32),
        scratch_shapes=[pltpu.VMEM((KV_HEADS, n_blk, qb, PAIR), BF16),
                        pltpu.VMEM((KV_HEADS, n_blk, PAIR, qb), BF16),
                        pltpu.VMEM((KV_HEADS, n_past, PAIR), BF16),
                        pltpu.VMEM((KV_HEADS, PAIR, n_past), BF16)],
        compiler_params=_params(1, 1),
        name="attn_lat",
    )(q, kv, ck, cv, cos_t, sin_t, sink_rows)


def _postmix_kernel(x_ref, octx_ref, olat_ref, *rest, with_router, n_ctx_tiles, per_slot):
    y_fwd = rest[:SCAN_SLOTS]
    y_bwd = rest[SCAN_SLOTS:2 * SCAN_SLOTS]
    (bonus_ref, g_ref, mod_ref, gpost_ref, gpre_ref, lng_ref, lnb_ref, wo_ref, bd_ref, router_ref,
     x1_ref, h_ref, gates_ref) = rest[2 * SCAN_SLOTS:]
    m = mod_ref[0]
    bd = bd_ref[...]
    tile = lax.broadcasted_iota(jnp.int32, octx_ref.shape, 0) * 0 + pl.program_id(0)
    o_att = jnp.where(tile < n_ctx_tiles, octx_ref[...], olat_ref[...])
    slot = (lax.broadcasted_iota(jnp.int32, y_fwd[0].shape, 0) * 0 + pl.program_id(0)) // per_slot
    y = y_fwd[0][...] + y_bwd[0][...]
    for sl in range(1, SCAN_SLOTS):
        y = jnp.where(slot == sl, y_fwd[sl][...] + y_bwd[sl][...], y)
    inv_n = 1.0 / R_HEAD
    mu = _head_sum(y, bd) * inv_n
    yc = y - mu
    var = _head_sum(yc * yc, bd) * inv_n
    yn = yc * lax.rsqrt(var + GN_EPS)
    o_rw = (yn * lng_ref[...] + lnb_ref[...] + bonus_ref[...]) * g_ref[...]
    o = (jnp.dot(o_att.astype(BF16), wo_ref[:ATT_Q, :], preferred_element_type=F32)
         + jnp.dot(o_rw.astype(BF16), wo_ref[ATT_Q:, :], preferred_element_type=F32))
    x1 = x_ref[...] + m[2:3] * (_rms(o) * gpost_ref[...])
    x1_ref[...] = x1
    h = (_rms(x1) * gpre_ref[...]) * (1.0 + m[4:5]) + m[3:4]
    h_ref[...] = h.astype(BF16)
    if with_router:
        h_hi, h_lo = _split2(h)
        r_hi, r_lo = _split2(router_ref[...])
        d = functools.partial(jnp.dot, preferred_element_type=F32)
        logits = d(h_hi, r_hi) + (d(h_hi, r_lo) + d(h_lo, r_hi))
        lane = lax.broadcasted_iota(jnp.int32, logits.shape, 1)
        valid = lane < N_EXPERTS
        logits = jnp.where(valid, logits, NEG_INF)
        e = jnp.exp(logits - jnp.max(logits, -1, keepdims=True))
        probs = e / jnp.sum(e, -1, keepdims=True)
        lane_f = lane.astype(F32)
        p1 = jnp.max(probs, -1, keepdims=True)
        i1 = jnp.min(jnp.where(probs == p1, lane_f, 1e9), -1, keepdims=True)
        rest = jnp.where(lane_f == i1, -1.0, probs)
        p2 = jnp.max(rest, -1, keepdims=True)
        i2 = jnp.min(jnp.where(rest == p2, lane_f, 1e9), -1, keepdims=True)
        sel = (lane_f == i1) | (lane_f == i2)
        gates_ref[...] = jnp.where(sel, probs / (p1 + p2), 0.0)
    else:
        gates_ref[...] = jnp.ones(gates_ref.shape, F32)


def _postmix(x, o_ctx, o_lat, y0, y1, bonus, g, mods, p, w_o, layer, consts, mod_map,
             router_pad, router_layer, n_ctx_tiles):
    nt = x.shape[0]
    row = lambda i: (i, 0)
    c2 = lambda i: (0, 0)
    with_router = router_layer is not None
    per_slot = nt // TILE // SCAN_SLOTS
    kern = functools.partial(_postmix_kernel, with_router=with_router, n_ctx_tiles=n_ctx_tiles,
                             per_slot=per_slot)
    n_lat_tiles = nt // TILE - n_ctx_tiles
    y_specs = [pl.BlockSpec((TILE, RW),
                            lambda i, sl=sl: (jnp.clip(i - sl * per_slot, 0, per_slot - 1), 0))
               for sl in range(SCAN_SLOTS)]
    return pl.pallas_call(
        kern,
        grid=(nt // TILE,),
        in_specs=[pl.BlockSpec((TILE, D_MODEL), row),
                  pl.BlockSpec((TILE, ATT_Q), lambda i: (jnp.minimum(i, n_ctx_tiles - 1), 0)),
                  pl.BlockSpec((TILE, ATT_Q),
                               lambda i: (jnp.clip(i - n_ctx_tiles, 0, n_lat_tiles - 1), 0)),
                  *y_specs, *y_specs,
                  pl.BlockSpec((TILE, RW), row),
                  pl.BlockSpec((TILE, RW), row),
                  pl.BlockSpec((None, 1, 6, D_MODEL), lambda i: (layer, mod_map(i), 0, 0)),
                  _at_layer((1, D_MODEL), layer),
                  _at_layer((1, D_MODEL), layer),
                  _at_layer((1, RW), layer),
                  _at_layer((1, RW), layer),
                  _at_layer((ATT_Q + RW, D_MODEL), layer),
                  pl.BlockSpec((RW, RW), c2),
                  _at_layer((D_MODEL, 128), router_layer or 0)],
        out_specs=[pl.BlockSpec((TILE, D_MODEL), row),
                   pl.BlockSpec((TILE, D_MODEL), row),
                   pl.BlockSpec((TILE, 128), row)],
        out_shape=[jax.ShapeDtypeStruct((nt, D_MODEL), F32),
                   jax.ShapeDtypeStruct((nt, D_MODEL), BF16),
                   jax.ShapeDtypeStruct((nt, 128), F32)],
        compiler_params=_params(1),
        name="postmix",
    )(x, o_ctx, o_lat, *y0, *y1, bonus, g, mods, p["g_post_mix"], p["g_pre_ffn"],
      p["ln_g"], p["ln_b"], w_o, consts["bd"], router_pad)


def _ffn_kernel(h_ref, x1_ref, gates_ref, mod_ref, gpost_ref, w1_ref, w3_ref, w2_ref, o_ref,
                *, n_blocks, experts_per_block):
    blk = pl.program_id(1)
    per_expert = w1_ref.shape[-1]
    width = experts_per_block * per_expert

    def cols(w_ref, lo, hi):
        pieces = []
        for j in range(experts_per_block):
            a, b = max(lo, j * per_expert), min(hi, (j + 1) * per_expert)
            if a < b:
                pieces.append(w_ref[j, :, a - j * per_expert:b - j * per_expert])
        return pieces[0] if len(pieces) == 1 else jnp.concatenate(pieces, 1)

    @pl.when(blk == 0)
    def _():
        o_ref[...] = jnp.zeros_like(o_ref)

    h = h_ref[...]
    lane = lax.broadcasted_iota(jnp.int32, gates_ref.shape, 1)
    gate = [jnp.sum(jnp.where(lane == blk * experts_per_block + j, gates_ref[...], 0.0), -1,
                    keepdims=True) for j in range(experts_per_block)]
    out = None
    for lo in range(0, width, FF_SUB):
        hi = min(lo + FF_SUB, width)
        a = jnp.dot(h, cols(w1_ref, lo, hi), preferred_element_type=F32)
        b = jnp.dot(h, cols(w3_ref, lo, hi), preferred_element_type=F32)
        col = lo + lax.broadcasted_iota(jnp.int32, (1, hi - lo), 1)
        g_col = gate[-1]
        for j in range(experts_per_block - 2, -1, -1):
            g_col = jnp.where(col < (j + 1) * per_expert, gate[j], g_col)
        t = ((a * _sigmoid(a)) * b) * g_col
        part = jnp.dot(t.astype(BF16), w2_ref[0, lo:hi, :], preferred_element_type=F32)
        out = part if out is None else out + part
    o_ref[...] += out

    @pl.when(blk == n_blocks - 1)
    def _():
        m = mod_ref[0]
        o_ref[...] = x1_ref[...] + m[5:6] * (_rms(o_ref[...]) * gpost_ref[...])


def _ffn(h, x1, gates, mods, g_post, w1, w3, w2, layer, w_layer, experts_per_block, mod_map_ffn):
    nt = h.shape[0]
    _, n_experts, _, per_expert = w1.shape
    n_blocks = n_experts // experts_per_block
    width = experts_per_block * per_expert
    up_block = (None, experts_per_block, D_MODEL, per_expert)
    row = lambda i, b: (i, 0)
    kern = functools.partial(_ffn_kernel, n_blocks=n_blocks, experts_per_block=experts_per_block)
    return pl.pallas_call(
        kern,
        grid=(nt // FFN_TILE, n_blocks),
        in_specs=[pl.BlockSpec((FFN_TILE, D_MODEL), row),
                  pl.BlockSpec((FFN_TILE, D_MODEL), row),
                  pl.BlockSpec((FFN_TILE, 128), row),
                  pl.BlockSpec((None, 1, 6, D_MODEL), lambda i, b: (layer, mod_map_ffn(i), 0, 0)),
                  _at_layer((1, D_MODEL), layer),
                  pl.BlockSpec(up_block, lambda i, b: (w_layer, b, 0, 0)),
                  pl.BlockSpec(up_block, lambda i, b: (w_layer, b, 0, 0)),
                  pl.BlockSpec((None, 1, width, D_MODEL), lambda i, b: (w_layer, b, 0, 0))],
        out_specs=pl.BlockSpec((FFN_TILE, D_MODEL), row),
        out_shape=jax.ShapeDtypeStruct((nt, D_MODEL), F32),
        compiler_params=_params(1, 1, FFN_VMEM_LIMIT),
        name="ffn",
    )(h, x1, gates, mods, g_post, w1, w3, w2)


def _rope_tables(seq_len):
    half = HEAD_DIM // 2
    nf = half // 2
    inv = ROPE_THETA ** (-jnp.arange(nf, dtype=F32) / nf)
    t = jnp.arange(seq_len)
    row = (t // GRID_W).astype(F32)
    col = (t % GRID_W).astype(F32)
    lane = jnp.arange(PAIR)
    pos = jnp.where(((lane % HEAD_DIM) // half == 0)[None, :], row[:, None], col[:, None])
    ang = pos * inv[lane % nf][None, :]
    first = ((lane % half) < nf)[None, :]
    return jnp.cos(ang), jnp.where(first, -jnp.sin(ang), jnp.sin(ang))


def _constants():
    r = jnp.arange(RW)
    bd = (r[:, None] // R_HEAD == r[None, :] // R_HEAD).astype(BF16)
    t = jnp.arange(TILE)
    same = t[:, None] // CHUNK == t[None, :] // CHUNK
    tri = jnp.stack([same & (t[None, :] <= t[:, None]), same & (t[None, :] >= t[:, None])])
    return {"bd": bd, "tri": tri.astype(BF16)}


def _pad_rows(w2):
    z = jnp.zeros_like(w2[:, 0])
    return jnp.stack([jnp.concatenate([w2[:, 0], z], 1), jnp.concatenate([z, w2[:, 1]], 1)],
                     1).astype(BF16)


def kernel(x_prompt, x_sample, cache_k, cache_v, state_rwkv, c, c_ctx, w_mod, b_mod, g_pre_mix, g_post_mix, g_pre_ffn, g_post_ffn, w_in, mu_shift, w_o, attn_sink, rw_w0, rw_w1, rw_w2, rw_a0, rw_a1, rw_a2, rw_g1, rw_g2, rw_k_k, rw_k_a, rw_r_k, rw_ln_g, rw_ln_b, ffn_w1, ffn_w3, ffn_w2, moe_router, moe_w1, moe_w3, moe_w2):
    n_ctx, s_ctx, _ = x_prompt.shape
    n_lat, s_lat, _ = x_sample.shape
    n_past = cache_k.shape[2]
    assert s_ctx == TILE and s_lat % TILE == 0 and (n_ctx * s_ctx) % s_lat == 0
    nt_ctx = n_ctx * s_ctx
    n_ctx_tiles = nt_ctx // TILE
    tiles_per_lat = s_lat // TILE
    mod_map = functools.partial(_mod_row, n_ctx_tiles=n_ctx_tiles, tiles_per_lat=tiles_per_lat)
    ffn_per = FFN_TILE // TILE
    mod_map_ffn = lambda i: mod_map(i * ffn_per)

    x = jnp.concatenate([x_prompt.reshape(nt_ctx, D_MODEL), x_sample.reshape(n_lat * s_lat, D_MODEL)], 0)
    n_cond = 8
    cond = jnp.concatenate([c_ctx[None, :], c, jnp.zeros((n_cond - 1 - n_lat, D_MODEL), F32)], 0)
    mods = _modulation(cond, w_mod, b_mod).reshape(DEPTH, n_cond, 6, D_MODEL)

    consts = _constants()
    cos_t, sin_t = _rope_tables(s_lat)
    w_cat = jnp.concatenate(
        [w_in, rw_w1[:, 0], rw_w1[:, 1], rw_a1[:, 0], rw_a1[:, 1], rw_g1], 2).astype(BF16)
    w_o_b = w_o.astype(BF16)
    dense_w = [w.astype(BF16)[:, None] for w in (ffn_w1, ffn_w3, ffn_w2)]
    moe_w = [moe_w1.astype(BF16), moe_w3.astype(BF16),
             moe_w2.astype(BF16).reshape(moe_w2.shape[0], N_EXPERTS // 2, 2 * D_FF_EXPERT, D_MODEL)]
    router_pad = jnp.pad(moe_router, ((0, 0), (0, 0), (0, 128 - N_EXPERTS)))
    sink_rows = jnp.broadcast_to(attn_sink[:, :, None], (DEPTH, N_HEADS, 128))
    ck = cache_k.reshape(n_lat, DEPTH, n_past, ATT_KV)
    cv = cache_v.reshape(n_lat, DEPTH, n_past, ATT_KV)

    vecs = lambda a: a.reshape(DEPTH, 1, -1)
    p = {"mu": vecs(mu_shift), "w0": rw_w0, "a0": rw_a0, "w2": _pad_rows(rw_w2),
         "a2": _pad_rows(rw_a2), "g2": rw_g2.astype(BF16), "k_k": vecs(rw_k_k),
         "k_a": vecs(rw_k_a), "r_k": vecs(rw_r_k), "ln_g": vecs(rw_ln_g), "ln_b": vecs(rw_ln_b),
         "g_post_mix": vecs(g_post_mix), "g_pre_ffn": vecs(g_pre_ffn)}
    g_pre_mix_v = vecs(g_pre_mix)
    g_post_ffn_v = vecs(g_post_ffn)

    new_k, new_v, new_s = [], [], []
    for l in range(DEPTH):
        q, kv, v_b, pk, wl, bonus, g = _inprep(
            x, mods, g_pre_mix_v, w_cat, p, l, consts, mod_map, n_ctx_tiles, tiles_per_lat)
        y0, y1, s_ctx_new = _scan(pk, v_b, wl, state_rwkv[:, l],
                                  n_ctx_tiles, tiles_per_lat)
        o_ctx = _attn_ctx(q, kv, sink_rows, l, n_ctx, s_ctx)
        o_lat = _attn_lat(q, kv, ck, cv, l, cos_t, sin_t, sink_rows, nt_ctx, n_lat, s_lat)

        i = l // 2
        moe = l % 2 == 1
        x1, h, gates = _postmix(x, o_ctx, o_lat, y0, y1, bonus, g, mods, p, w_o_b, l, consts,
                                mod_map, router_pad, i if moe else None, n_ctx_tiles)
        w1, w3, w2 = moe_w if moe else dense_w
        x = _ffn(h, x1, gates, mods, g_post_ffn_v, w1, w3, w2, l, i, 2 if moe else 1,
                 mod_map_ffn)

        new_k.append(kv[:nt_ctx, :ATT_KV].reshape(n_ctx, s_ctx, KV_HEADS, HEAD_DIM))
        new_v.append(kv[:nt_ctx, ATT_KV:].reshape(n_ctx, s_ctx, KV_HEADS, HEAD_DIM))
        new_s.append(s_ctx_new)

    y_prompt = x[:nt_ctx].reshape(n_ctx, s_ctx, D_MODEL)
    y_sample = x[nt_ctx:].reshape(n_lat, s_lat, D_MODEL)
    return (y_prompt, y_sample, jnp.stack(new_k, 1), jnp.stack(new_v, 1), jnp.stack(new_s, 1))
```
